```python
import math
import jax
import jax.numpy as jnp
from jax import lax
import numpy as np

D_MODEL = 1024
BATCH = 32
SEQ = 2048
DEPTH = 2
DEC_BATCH = 8
DEC_SEQ = 16
PAST_LEN = 1024

CHUNK = 64

SSD_D_INNER = 1024
SSD_HEAD_DIM = 64
SSD_HEADS = SSD_D_INNER // SSD_HEAD_DIM
SSD_GROUPS = 2
SSD_HEADS_PER_GROUP = SSD_HEADS // SSD_GROUPS
SSD_STATE = 128
SSD_CONV_W = 4
SSD_CONV_DIM = SSD_D_INNER + 2 * SSD_GROUPS * SSD_STATE
SSD_BLOCK = CHUNK

POOL_DIM = 1024
POOL_WINDOWS = (2, 4, 8, 16)
POOL_GROUPS = 4
POOL_GROUP_DIM = POOL_DIM // POOL_GROUPS
POOL_MAXW = 16

IN_L0 = SSD_D_INNER + SSD_CONV_DIM + SSD_HEADS + POOL_DIM
MIX_L0 = SSD_D_INNER + POOL_DIM

HGRN_EXPAND = 128
HGRN_HEADS = D_MODEL // HGRN_EXPAND
HGRN_DK = HGRN_EXPAND
HGRN_DV = D_MODEL // HGRN_HEADS
HGRN_DIM = HGRN_HEADS * HGRN_DK
HGRN_BLOCK = 16

N_EXPERTS = 64
N_EXPERT_GROUPS = 8
TOPK_GROUPS = 4
TOP_K = 8
EXPERT_FF = 256
ROUTED_SCALE = 2.5
MOE_BLOCK = 128

ALPHA = (2 * DEPTH) ** 0.25
BETA = (8 * DEPTH) ** -0.25
EPS = 1e-5
F32 = jnp.float32

kernel_name = 'hybrid_ssd_pool_hgrn2_moe_stream_step'


def layer_norm(x, g, b):
    xf = x.astype(F32)
    xc = xf - jnp.mean(xf, -1, keepdims=True)
    var = jnp.mean(xc * xc, -1, keepdims=True)
    return (xc * lax.rsqrt(var + EPS) * g.astype(F32) + b.astype(F32)).astype(x.dtype)


def rms_norm(x):
    return x * lax.rsqrt(jnp.mean(x * x, -1, keepdims=True) + EPS)


def pad_time(t, pad):
    return jnp.pad(t, [(0, 0), (0, pad)] + [(0, 0)] * (t.ndim - 2))


def ssd_scan(x, dt, a, bmat, cmat, d_skip, h0):
    bsz, L = x.shape[0], x.shape[1]
    pad = (-L) % SSD_BLOCK
    nc = (L + pad) // SSD_BLOCK
    G, R, P, N = SSD_GROUPS, SSD_HEADS_PER_GROUP, SSD_HEAD_DIM, SSD_STATE
    xc = pad_time(x, pad).reshape(bsz, nc, SSD_BLOCK, G, R, P)
    dtc = pad_time(dt, pad).reshape(bsz, nc, SSD_BLOCK, G, R)
    bc = pad_time(bmat, pad).reshape(bsz, nc, SSD_BLOCK, G, N)
    cc = pad_time(cmat, pad).reshape(bsz, nc, SSD_BLOCK, G, N)
    cum = jnp.cumsum(dtc * a.reshape(G, R), axis=2)
    causal = jnp.tril(jnp.ones((SSD_BLOCK, SSD_BLOCK), bool))[:, :, None, None]
    seg = cum[:, :, :, None] - cum[:, :, None, :]
    decay = jnp.exp(jnp.where(causal, seg, -jnp.inf))
    cb = jnp.einsum('bclgn,bcsgn->bclsg', cc, bc)
    mix = cb[..., None] * decay * dtc[:, :, None]
    y_diag = jnp.einsum('bclsgr,bcsgrp->bclgrp', mix, xc)
    to_end = jnp.exp(cum[:, :, -1:] - cum) * dtc
    block_states = jnp.einsum('bclgn,bclgr,bclgrp->bcgrpn', bc, to_end, xc)
    block_decay = jnp.exp(cum[:, :, -1])

    def step(h, inp):
        dec, st = inp
        return dec[..., None, None] * h + st, h

    h_final, h_prev = lax.scan(step, h0.reshape(bsz, G, R, P, N),
                               (block_decay.swapaxes(0, 1), block_states.swapaxes(0, 1)))
    h_prev = h_prev.swapaxes(0, 1)
    y_off = jnp.einsum('bclgn,bcgrpn,bclgr->bclgrp', cc, h_prev, jnp.exp(cum))
    y = y_diag + y_off + xc * d_skip.reshape(G, R)[..., None]
    y = y.reshape(bsz, nc * SSD_BLOCK, SSD_HEADS, P)[:, :L]
    return y, h_final.reshape(bsz, SSD_HEADS, P, N)


def ssd_mixer(z, xbc, dt_raw, conv_state, ssm_state, conv_w, conv_b, dt_bias, a_log, d_skip, norm_w):
    bsz, L, _ = xbc.shape
    xpad = jnp.concatenate([conv_state.astype(xbc.dtype), xbc], axis=1)
    new_conv = xpad[:, -(SSD_CONV_W - 1):]
    conv = lax.conv_general_dilated(xpad, conv_w[:, None, :].astype(xbc.dtype), window_strides=(1,),
                                    padding='VALID', dimension_numbers=('NWC', 'WIO', 'NWC'),
                                    feature_group_count=SSD_CONV_DIM)
    xbc = jax.nn.silu(conv + conv_b.astype(xbc.dtype))
    xs, bm, cm = jnp.split(xbc, [SSD_D_INNER, SSD_D_INNER + SSD_GROUPS * SSD_STATE], axis=-1)
    dt = jax.nn.softplus(dt_raw.astype(F32) + dt_bias.astype(F32))
    a = -jnp.exp(a_log.astype(F32))
    y, h = ssd_scan(xs.reshape(bsz, L, SSD_HEADS, SSD_HEAD_DIM).astype(F32), dt, a,
                    bm.reshape(bsz, L, SSD_GROUPS, SSD_STATE).astype(F32),
                    cm.reshape(bsz, L, SSD_GROUPS, SSD_STATE).astype(F32),
                    d_skip.astype(F32), ssm_state.astype(F32))
    y = y.reshape(bsz, L, SSD_D_INNER) * jax.nn.silu(z.astype(F32))
    y = rms_norm(y.reshape(bsz, L, SSD_GROUPS, -1)).reshape(bsz, L, SSD_D_INNER) * norm_w.astype(F32)
    return y.astype(z.dtype), new_conv, h.astype(z.dtype)


def pool_mixer(u, pool_state, pos0, w_pool, pool_scale):
    bsz, L, _ = u.shape
    upad = jnp.concatenate([pool_state.astype(u.dtype), u], axis=1)
    new_state = upad[:, -(POOL_MAXW - 1):]
    cs = jnp.cumsum(upad.astype(F32), axis=1)
    cs = jnp.concatenate([jnp.zeros((bsz, 1, POOL_DIM), F32), cs], axis=1)
    off = POOL_MAXW
    pos = (pos0 + jnp.arange(L)).astype(F32)
    means = []
    for gi, w in enumerate(POOL_WINDOWS):
        sl = slice(gi * POOL_GROUP_DIM, (gi + 1) * POOL_GROUP_DIM)
        win_sum = cs[:, off:off + L, sl] - cs[:, off - w:off - w + L, sl]
        cnt = jnp.minimum(pos + 1.0, float(w))
        means.append(win_sum / cnt[None, :, None])
    pooled = jnp.concatenate(means, axis=-1)
    diff = (pooled - u.astype(F32)).reshape(bsz, L, POOL_GROUPS, POOL_GROUP_DIM)
    y = jnp.einsum('blgc,gcd->blgd', diff, w_pool.astype(F32)).reshape(bsz, L, POOL_DIM)
    return (y * pool_scale.astype(F32)).astype(u.dtype), new_state


def hgrn2_mixer(q_raw, f_raw, v_raw, g_raw, state, lb, norm_w):
    bsz, L, _ = q_raw.shape
    lbf = lb.astype(F32)
    fr = f_raw.astype(F32)
    log_f = jnp.logaddexp(jnp.log(lbf), jnp.log1p(-lbf) + jax.nn.log_sigmoid(fr))
    k = (1.0 - lbf) * jax.nn.sigmoid(-fr)
    q = jax.nn.silu(q_raw.astype(F32))
    pad = (-L) % HGRN_BLOCK
    nb = (L + pad) // HGRN_BLOCK

    def blocks(t, d):
        t = pad_time(t, pad)
        return t.reshape(bsz, nb, HGRN_BLOCK, HGRN_HEADS, d).transpose(1, 0, 3, 2, 4)

    qb, kb, lfb = blocks(q, HGRN_DK), blocks(k, HGRN_DK), blocks(log_f, HGRN_DK)
    vb = blocks(v_raw.astype(F32), HGRN_DV)
    causal = jnp.tril(jnp.ones((HGRN_BLOCK, HGRN_BLOCK), bool))[:, :, None]

    def step(s, inp):
        qi, ki, lfi, vi = inp
        gcum = jnp.cumsum(lfi, axis=2)
        decay = jnp.exp(jnp.where(causal, gcum[:, :, :, None] - gcum[:, :, None, :], -jnp.inf))
        scores = jnp.einsum('bhlk,bhsk,bhlsk->bhls', qi, ki, decay)
        o = jnp.einsum('bhls,bhsv->bhlv', scores, vi) + jnp.einsum('bhlk,bhkv->bhlv', qi * jnp.exp(gcum), s)
        glast = gcum[:, :, -1:]
        s = jnp.exp(glast[:, :, 0])[..., None] * s + jnp.einsum('bhsk,bhsv->bhkv', ki * jnp.exp(glast - gcum), vi)
        return s, o

    s_final, ob = lax.scan(step, state.astype(F32), (qb, kb, lfb, vb))
    o = ob.transpose(1, 0, 3, 2, 4).reshape(bsz, nb * HGRN_BLOCK, HGRN_HEADS, HGRN_DV)[:, :L]
    gate = jax.nn.silu(g_raw.astype(F32).reshape(bsz, L, HGRN_HEADS, HGRN_DV))
    o = rms_norm(o) * norm_w.astype(F32) * gate
    return o.reshape(bsz, L, HGRN_HEADS * HGRN_DV).astype(q_raw.dtype), s_final.astype(q_raw.dtype)


def moe_route(x2, w_r, b_r):
    T = x2.shape[0]
    scores = jax.nn.sigmoid(x2.astype(F32) @ w_r.astype(F32))
    sel = (scores + b_r.astype(F32)).reshape(T, N_EXPERT_GROUPS, N_EXPERTS // N_EXPERT_GROUPS)
    grp_score = lax.top_k(sel, 2)[0].sum(-1)
    _, top_g = lax.top_k(grp_score, TOPK_GROUPS)
    keep = jnp.any(top_g[:, :, None] == jnp.arange(N_EXPERT_GROUPS), axis=1)
    sel = jnp.where(keep[:, :, None], sel, -jnp.inf).reshape(T, N_EXPERTS)
    _, idx = lax.top_k(sel, TOP_K)
    w = jnp.take_along_axis(scores, idx, axis=-1)
    gates = w / jnp.sum(w, -1, keepdims=True) * ROUTED_SCALE
    return idx, gates


def moe_ffn(x, w_r, b_r, w_in, w_out, ws_in, ws_out):
    bsz, L, D = x.shape
    T = bsz * L
    x2 = x.reshape(T, D)
    idx, gates = moe_route(x2, w_r, b_r)
    n_assign = T * TOP_K
    flat_e = idx.reshape(-1)
    flat_tok = jnp.arange(n_assign, dtype=jnp.int32) // TOP_K
    flat_g = gates.reshape(-1)
    order = jnp.argsort(flat_e)
    se = flat_e[order]
    counts = jnp.bincount(flat_e, length=N_EXPERTS)
    padded = (counts + MOE_BLOCK - 1) // MOE_BLOCK * MOE_BLOCK
    pad_end = jnp.cumsum(padded)
    pad_start = pad_end - padded
    start = jnp.cumsum(counts) - counts
    dest = pad_start[se] + jnp.arange(n_assign, dtype=jnp.int32) - start[se]
    n_blocks = (n_assign + N_EXPERTS * (MOE_BLOCK - 1) + MOE_BLOCK - 1) // MOE_BLOCK
    n_rows = n_blocks * MOE_BLOCK
    row_tok = jnp.full((n_rows,), T, jnp.int32).at[dest].set(flat_tok[order])
    row_gate = jnp.zeros((n_rows,), x.dtype).at[dest].set(flat_g[order].astype(x.dtype))
    blk_start = jnp.arange(n_blocks, dtype=jnp.int32) * MOE_BLOCK
    blk_e = jnp.minimum(jnp.searchsorted(pad_end, blk_start, side='right'), N_EXPERTS - 1)
    x_ext = jnp.concatenate([x2, jnp.zeros((1, D), x.dtype)], axis=0)

    def expert_block(acc, blk):
        e, tok, g = blk
        h = x_ext[tok]
        a, u = jnp.split(h @ w_in[e], 2, axis=-1)
        out = ((jax.nn.silu(a) * u) @ w_out[e]) * g[:, None]
        return acc.at[tok].add(out), None

    routed, _ = lax.scan(expert_block, jnp.zeros((T + 1, D), x.dtype),
                         (blk_e, row_tok.reshape(n_blocks, MOE_BLOCK), row_gate.reshape(n_blocks, MOE_BLOCK)))
    a, u = jnp.split(x2 @ ws_in, 2, axis=-1)
    shared = (jax.nn.silu(a) * u) @ ws_out
    return (routed[:T] + shared).reshape(bsz, L, D)


def trunk(x, pos0, conv_st, ssm_st, pool_st, hgrn_st, weights):
    (w_in_l0, conv_w_l0, conv_b_l0, dt_bias_l0, a_log_l0, d_skip_l0, ssd_norm_l0, w_pool_l0,
     pool_scale_l0, w_out_l0, w_in_l1, lb_raw, hgrn_norm_l1, w_out_l1, ln_mix_g, ln_mix_b,
     ln_ffn_g, ln_ffn_b, router_w, router_bias, moe_w_in, moe_w_out, shared_w_in, shared_w_out) = weights
    lb_sm = jax.nn.softmax(lb_raw.astype(F32), axis=0)
    lb_all = jnp.cumsum(lb_sm, axis=0) - lb_sm[0]
    for layer in range(DEPTH):
        if layer % 2 == 0:
            proj = x @ w_in_l0
            z, xbc, dt_raw, u = jnp.split(
                proj, [SSD_D_INNER, SSD_D_INNER + SSD_CONV_DIM, SSD_D_INNER + SSD_CONV_DIM + SSD_HEADS], axis=-1)
            y_a, conv_st, ssm_st = ssd_mixer(z, xbc, dt_raw, conv_st, ssm_st, conv_w_l0, conv_b_l0,
                                             dt_bias_l0, a_log_l0, d_skip_l0, ssd_norm_l0)
            y_b, pool_st = pool_mixer(u, pool_st, pos0, w_pool_l0, pool_scale_l0)
            mix = jnp.concatenate([y_a, y_b], axis=-1) @ w_out_l0
        else:
            proj = x @ w_in_l1
            q, f, v, g = jnp.split(proj, 4, axis=-1)
            o, hgrn_st = hgrn2_mixer(q, f, v, g, hgrn_st, lb_all[layer], hgrn_norm_l1)
            mix = o @ w_out_l1
        x = layer_norm(ALPHA * x + mix, ln_mix_g[layer], ln_mix_b[layer])
        ffn = moe_ffn(x, router_w[layer], router_bias[layer], moe_w_in[layer], moe_w_out[layer],
                      shared_w_in[layer], shared_w_out[layer])
        x = layer_norm(ALPHA * x + ffn, ln_ffn_g[layer], ln_ffn_b[layer])
    return x, conv_st, ssm_st, pool_st, hgrn_st


def setup_inputs(seed: int = 0) -> dict:
    key = jax.random.key(seed)
    ks = jax.random.split(key, 32)

    def nrm(k, shape, scale):
        return jax.random.normal(k, shape, jnp.float32) * scale

    dt0 = jnp.exp(jax.random.uniform(ks[9], (SSD_HEADS,), jnp.float32, math.log(1e-3), math.log(1e-1)))
    return {
        'x_prompt': nrm(ks[0], (BATCH, SEQ, D_MODEL), 1.0),
        'x_sample': nrm(ks[1], (DEC_BATCH, DEC_SEQ, D_MODEL), 1.0),
        'state_conv_l0': nrm(ks[2], (DEC_BATCH, SSD_CONV_W - 1, SSD_CONV_DIM), 1.0),
        'state_ssm_l0': nrm(ks[3], (DEC_BATCH, SSD_HEADS, SSD_HEAD_DIM, SSD_STATE), 0.3),
        'state_pool_l0': nrm(ks[4], (DEC_BATCH, POOL_MAXW - 1, POOL_DIM), 1.0),
        'state_hgrn_l1': nrm(ks[5], (DEC_BATCH, HGRN_HEADS, HGRN_DK, HGRN_DV), 0.3),
        'w_in_l0': nrm(ks[6], (D_MODEL, IN_L0), D_MODEL ** -0.5),
        'conv_w_l0': nrm(ks[7], (SSD_CONV_W, SSD_CONV_DIM), SSD_CONV_W ** -0.5),
        'conv_b_l0': nrm(ks[8], (SSD_CONV_DIM,), 0.02),
        'dt_bias_l0': dt0 + jnp.log(-jnp.expm1(-dt0)),
        'a_log_l0': jnp.log(jax.random.uniform(ks[10], (SSD_HEADS,), jnp.float32, 1.0, 16.0)),
        'd_skip_l0': 1.0 + nrm(ks[11], (SSD_HEADS,), 0.1),
        'ssd_norm_l0': 1.0 + nrm(ks[12], (SSD_D_INNER,), 0.05),
        'w_pool_l0': nrm(ks[13], (POOL_GROUPS, POOL_GROUP_DIM, POOL_GROUP_DIM), POOL_GROUP_DIM ** -0.5),
        'pool_scale_l0': 1.0 + nrm(ks[14], (POOL_DIM,), 0.1),
        'w_out_l0': nrm(ks[15], (MIX_L0, D_MODEL), MIX_L0 ** -0.5 * BETA),
        'w_in_l1': nrm(ks[16], (D_MODEL, 4 * HGRN_DIM), D_MODEL ** -0.5),
        'lb_raw': nrm(ks[17], (DEPTH, HGRN_DIM), 0.5),
        'hgrn_norm_l1': 1.0 + nrm(ks[18], (HGRN_DV,), 0.05),
        'w_out_l1': nrm(ks[19], (HGRN_DIM, D_MODEL), HGRN_DIM ** -0.5 * BETA),
        'ln_mix_g': 1.0 + nrm(ks[20], (DEPTH, D_MODEL), 0.05),
        'ln_mix_b': nrm(ks[21], (DEPTH, D_MODEL), 0.02),
        'ln_ffn_g': 1.0 + nrm(ks[22], (DEPTH, D_MODEL), 0.05),
        'ln_ffn_b': nrm(ks[23], (DEPTH, D_MODEL), 0.02),
        'router_w': nrm(ks[24], (DEPTH, D_MODEL, N_EXPERTS), D_MODEL ** -0.5),
        'router_bias': nrm(ks[25], (DEPTH, N_EXPERTS), 0.01),
        'moe_w_in': nrm(ks[26], (DEPTH, N_EXPERTS, D_MODEL, 2 * EXPERT_FF), D_MODEL ** -0.5),
        'moe_w_out': nrm(ks[27], (DEPTH, N_EXPERTS, EXPERT_FF, D_MODEL), EXPERT_FF ** -0.5 * BETA),
        'shared_w_in': nrm(ks[28], (DEPTH, D_MODEL, 2 * EXPERT_FF), D_MODEL ** -0.5),
        'shared_w_out': nrm(ks[29], (DEPTH, EXPERT_FF, D_MODEL), EXPERT_FF ** -0.5 * BETA),
    }


def reference(x_prompt, x_sample, state_conv_l0, state_ssm_l0, state_pool_l0, state_hgrn_l1,
              w_in_l0, conv_w_l0, conv_b_l0, dt_bias_l0, a_log_l0, d_skip_l0, ssd_norm_l0, w_pool_l0,
              pool_scale_l0, w_out_l0, w_in_l1, lb_raw, hgrn_norm_l1, w_out_l1, ln_mix_g, ln_mix_b,
              ln_ffn_g, ln_ffn_b, router_w, router_bias, moe_w_in, moe_w_out, shared_w_in, shared_w_out):
    weights = (w_in_l0, conv_w_l0, conv_b_l0, dt_bias_l0, a_log_l0, d_skip_l0, ssd_norm_l0, w_pool_l0,
               pool_scale_l0, w_out_l0, w_in_l1, lb_raw, hgrn_norm_l1, w_out_l1, ln_mix_g, ln_mix_b,
               ln_ffn_g, ln_ffn_b, router_w, router_bias, moe_w_in, moe_w_out, shared_w_in, shared_w_out)
    bp = x_prompt.shape[0]
    dtp = x_prompt.dtype
    zero_conv = jnp.zeros((bp, SSD_CONV_W - 1, SSD_CONV_DIM), dtp)
    zero_ssm = jnp.zeros((bp, SSD_HEADS, SSD_HEAD_DIM, SSD_STATE), dtp)
    zero_pool = jnp.zeros((bp, POOL_MAXW - 1, POOL_DIM), dtp)
    zero_hgrn = jnp.zeros((bp, HGRN_HEADS, HGRN_DK, HGRN_DV), dtp)
    y_prompt, p_conv, p_ssm, p_pool, p_hgrn = trunk(x_prompt, 0, zero_conv, zero_ssm, zero_pool,
                                                    zero_hgrn, weights)
    y_sample, s_conv, s_ssm, s_pool, s_hgrn = trunk(x_sample, PAST_LEN, state_conv_l0, state_ssm_l0,
                                                    state_pool_l0, state_hgrn_l1, weights)
    return (y_prompt, y_sample, p_conv, p_ssm, p_pool, p_hgrn, s_conv, s_ssm, s_pool, s_hgrn)
```

```python
import functools
import math

import jax
import jax.numpy as jnp
from jax import lax
from jax.experimental import pallas as pl
from jax.experimental.pallas import tpu as pltpu

F32 = jnp.float32
BF16 = jnp.bfloat16
I32 = jnp.int32
HIGHEST = lax.Precision.HIGHEST

D_MODEL = 1024
DEPTH = 2
ALPHA = (2 * DEPTH) ** 0.25
EPS = 1e-5

SSD_HEADS = 16
SSD_HEAD_DIM = 64
SSD_GROUPS = 2
SSD_STATE = 128
SSD_D_INNER = 1024
SSD_CONV_DIM = 1536
SSD_CONV_W = 4
SSD_Q = 64
GROUP_CH = SSD_D_INNER // SSD_GROUPS

POOL_DIM = 1024
POOL_WINDOWS = (2, 4, 8, 16)
POOL_GROUP_DIM = 256
POOL_MAXW = 16

HGRN_HEADS = 8
HGRN_DK = 128
HGRN_DV = 128
HGRN_C = 128
HGRN_SUB = 16

N_EXPERTS = 64
N_EXPERT_GROUPS = 8
GROUP_SIZE = N_EXPERTS // N_EXPERT_GROUPS
TOPK_GROUPS = 4
TOP_K = 8
EXPERT_FF = 256
ROUTED_SCALE = 2.5

V7X_VMEM_BYTES = 64 * 1024 * 1024
VMEM_LIMIT = V7X_VMEM_BYTES - 8 * 1024 * 1024
LANES = 128
SUBLANES = 8

NT_DIMS = (((1,), (1,)), ((), ()))
TN_DIMS = (((0,), (0,)), ((), ()))


def _dot(a, b, precision=None):
    return jnp.dot(a, b, preferred_element_type=F32, precision=precision)


def _dot_nt(a, b, precision=None):
    return lax.dot_general(a, b, NT_DIMS, preferred_element_type=F32, precision=precision)


def _dot_tn(a, b):
    return lax.dot_general(a, b, TN_DIMS, preferred_element_type=F32)


def _softplus(v):
    return jnp.maximum(v, 0.0) + jnp.log1p(jnp.exp(-jnp.abs(v)))


def _silu(v):
    return v * jax.nn.sigmoid(v)


def _layer_norm(h, g, b):
    mu = jnp.mean(h, axis=-1, keepdims=True)
    hc = h - mu
    var = jnp.mean(hc * hc, axis=-1, keepdims=True)
    return hc * lax.rsqrt(var + EPS) * g + b


def _iota(shape, dim):
    return lax.broadcasted_iota(I32, shape, dim)


def _l0_kernel(x_ref, conv0_ref, ssm0_ref, pool0_ref,
               w_in_ref, w_dtt_ref, conv_w_ref, conv_b_ref, dtb_x_ref, dtb_c_ref, a_x_ref, a_c_ref,
               dskip_ref, normw_ref, w_pool_ref, pscale_ref, w_out_ref, lng_ref, lnb_ref,
               y_ref, conv_o_ref, ssm_o_ref, pool_o_ref,
               xbc_buf, u_buf, xa_buf, z_buf, dtx_buf, h_ref, mix_buf,
               *, tl, l_total, l_valid, pos0):
    j = pl.program_id(1)
    nt = l_total // tl
    q = SSD_Q
    nq = tl // q
    masked = l_valid < l_total

    @pl.when(j == 0)
    def _():
        xbc_buf[0:8, :] = conv0_ref[0]
        u_buf[0:16, :] = pool0_ref[0]
        h_ref[...] = ssm0_ref[0]

    x = x_ref[0]
    xb = x.astype(BF16)

    z_buf[...] = _dot(xb, w_in_ref[:, 0:1024])
    xbc_buf[8:8 + tl, :] = _dot(xb, w_in_ref[:, 1024:2560])
    u_buf[16:16 + tl, :] = _dot(xb, w_in_ref[:, 2560:3584])
    dtx = _softplus(_dot(xb, w_in_ref[:, 3584:4608]) + dtb_x_ref[...])
    if masked:
        t_col = j * tl + _iota((tl, 1), 0)
        dtx = jnp.where(t_col < l_valid, dtx, 0.0)
    dtx_buf[...] = dtx

    acc = conv_b_ref[...] + conv_w_ref[3:4, :] * xbc_buf[8:8 + tl, :]
    for jj in range(SSD_CONV_W - 1):
        acc = acc + conv_w_ref[jj:jj + 1, :] * xbc_buf[5 + jj:5 + jj + tl, :]
    xa_buf[...] = _silu(acc)

    a_x = a_x_ref[...]
    a_c = a_c_ref[...]
    dskip = dskip_ref[...]
    normw = normw_ref[...]

    ltri = (_iota((q, q), 1) <= _iota((q, q), 0)).astype(F32)
    sp = _iota((q, LANES), 0)
    ln = _iota((q, LANES), 1)
    mt_left = ((ln < q) & (sp <= ln)).astype(F32)
    mt_right = ((ln >= q) & (sp <= ln - q)).astype(F32)
    causal2 = (ln % q) <= sp
    low_half = ln < q

    def chunk(c, carry):
        r0 = pl.multiple_of(c * q, q)
        rows = pl.ds(r0, q)
        xs = xa_buf[rows, 0:1024]
        dtc = dtx_buf[rows, :]
        cum = _dot(ltri, dtc * a_x, HIGHEST)
        ecum = jnp.exp(cum)
        cum_last = cum[q - 1:q, :]
        xw = xs * (jnp.exp(cum_last - cum) * dtc)
        xdt = xs * dtc

        xcb = x_ref[0, rows, :].astype(BF16)
        dtt = _softplus(_dot_nt(w_dtt_ref[...], xcb) + dtb_c_ref[...])
        if masked:
            t_row = j * tl + r0 + _iota((1, q), 1)
            dtt = jnp.where(t_row < l_valid, dtt, 0.0)
        dat = dtt * a_c
        cum_t2 = _dot(dat[0:8, :], mt_left, HIGHEST) + _dot(dat[8:16, :], mt_right, HIGHEST)

        pieces = []
        for g in range(SSD_GROUPS):
            gs = slice(g * GROUP_CH, (g + 1) * GROUP_CH)
            bg = xa_buf[rows, 1024 + g * SSD_STATE:1024 + (g + 1) * SSD_STATE].astype(BF16)
            cg = xa_buf[rows, 1280 + g * SSD_STATE:1280 + (g + 1) * SSD_STATE].astype(BF16)
            cb2 = _dot_nt(cg, jnp.concatenate([bg, bg], axis=0))
            h_g = h_ref[:, gs]
            y_off = _dot(cg, h_g.astype(BF16)) * ecum[:, gs]
            for pi in range(GROUP_CH // LANES):
                i = g * (GROUP_CH // LANES) + pi
                ls = slice(i * LANES, (i + 1) * LANES)
                seg = cum[:, ls] - cum_t2[i:i + 1, :]
                mixp = (cb2 * jnp.exp(jnp.where(causal2, seg, -jnp.inf))).astype(BF16)
                xp = xdt[:, ls]
                wblk = jnp.concatenate([jnp.where(low_half, xp, 0.0), jnp.where(low_half, 0.0, xp)],
                                       axis=0).astype(BF16)
                y_pair = (_dot(mixp, wblk) + y_off[:, pi * LANES:(pi + 1) * LANES]
                          + xs[:, ls] * dskip[:, ls])
                pieces.append(y_pair)
            h_ref[:, gs] = h_g * ecum[q - 1:q, gs] + _dot_tn(bg, xw[:, gs].astype(BF16))

        for g in range(SSD_GROUPS):
            gs = slice(g * GROUP_CH, (g + 1) * GROUP_CH)
            yg = jnp.concatenate(pieces[g * 4:(g + 1) * 4], axis=1) * _silu(z_buf[rows, gs])
            ms = jnp.mean(yg * yg, axis=-1, keepdims=True)
            mix_buf[rows, gs] = (yg * lax.rsqrt(ms + EPS) * normw[:, gs]).astype(BF16)
        return carry

    lax.fori_loop(0, nq, chunk, 0)

    pos = (pos0 + j * tl + _iota((tl, 1), 0)).astype(F32)
    for gi, w in enumerate(POOL_WINDOWS):
        cs = slice(gi * POOL_GROUP_DIM, (gi + 1) * POOL_GROUP_DIM)
        cur = u_buf[:, cs]
        d = 1
        while d < w:
            cur = cur + pltpu.roll(cur, d, 0)
            d *= 2
        cnt = jnp.minimum(pos + 1.0, float(w))
        diff = cur[16:, :] / cnt - u_buf[16:16 + tl, cs]
        yb = _dot(diff.astype(BF16), w_pool_ref[gi]) * pscale_ref[:, cs]
        mix_buf[:, SSD_D_INNER + gi * POOL_GROUP_DIM:SSD_D_INNER + (gi + 1) * POOL_GROUP_DIM] = yb.astype(BF16)

    mix = _dot(mix_buf[...], w_out_ref[...])
    y_ref[0] = _layer_norm(ALPHA * x + mix, lng_ref[...], lnb_ref[...])

    lv = l_valid - (nt - 1) * tl

    @pl.when(j == nt - 1)
    def _():
        conv_o_ref[0] = xbc_buf[lv:lv + 8, :]
        pool_o_ref[0] = u_buf[lv:lv + 16, :]
        ssm_o_ref[0] = h_ref[...]

    @pl.when(j < nt - 1)
    def _():
        xbc_buf[0:8, :] = xbc_buf[tl:tl + 8, :]
        u_buf[0:16, :] = u_buf[tl:tl + 16, :]


def _const_spec(shape):
    nd = len(shape)
    return pl.BlockSpec(shape, lambda *_: (0,) * nd)


def _l0_mixer(x, conv0, ssm0_t, pool0, params, *, tl, l_valid, pos0):
    bsz, l_total, _ = x.shape
    nt = l_total // tl
    per_b3 = lambda b, j: (b, 0, 0)
    in_specs = [
        pl.BlockSpec((1, tl, D_MODEL), lambda b, j: (b, j, 0)),
        pl.BlockSpec((1, 8, SSD_CONV_DIM), per_b3),
        pl.BlockSpec((1, SSD_STATE, SSD_D_INNER), per_b3),
        pl.BlockSpec((1, 16, POOL_DIM), per_b3),
    ] + [_const_spec(p.shape) for p in params]
    out_shape = (
        jax.ShapeDtypeStruct((bsz, l_total, D_MODEL), F32),
        jax.ShapeDtypeStruct((bsz, 8, SSD_CONV_DIM), F32),
        jax.ShapeDtypeStruct((bsz, SSD_STATE, SSD_D_INNER), F32),
        jax.ShapeDtypeStruct((bsz, 16, POOL_DIM), F32),
    )
    out_specs = (
        pl.BlockSpec((1, tl, D_MODEL), lambda b, j: (b, j, 0)),
        pl.BlockSpec((1, 8, SSD_CONV_DIM), per_b3),
        pl.BlockSpec((1, SSD_STATE, SSD_D_INNER), per_b3),
        pl.BlockSpec((1, 16, POOL_DIM), per_b3),
    )
    scratch = [
        pltpu.VMEM((8 + tl, SSD_CONV_DIM), F32),
        pltpu.VMEM((16 + tl, POOL_DIM), F32),
        pltpu.VMEM((tl, SSD_CONV_DIM), F32),
        pltpu.VMEM((tl, SSD_D_INNER), F32),
        pltpu.VMEM((tl, SSD_D_INNER), F32),
        pltpu.VMEM((SSD_STATE, SSD_D_INNER), F32),
        pltpu.VMEM((tl, 2 * D_MODEL), BF16),
    ]
    return pl.pallas_call(
        functools.partial(_l0_kernel, tl=tl, l_total=l_total, l_valid=l_valid, pos0=pos0),
        grid=(bsz, nt),
        in_specs=in_specs,
        out_specs=out_specs,
        out_shape=out_shape,
        scratch_shapes=scratch,
        compiler_params=pltpu.CompilerParams(
            dimension_semantics=("arbitrary", "arbitrary"), vmem_limit_bytes=VMEM_LIMIT),
        name="l0_mixer",
    )(x, conv0, ssm0_t, pool0, *params)


def _l0_params(w_in, conv_w, conv_b, dt_bias, a_log, d_skip, ssd_norm, w_pool, pool_scale, w_out, ln_g, ln_b):
    rep = lambda v: jnp.repeat(v.astype(F32), SSD_HEAD_DIM)[None, :]
    z0, z1 = 0, SSD_D_INNER
    x1 = z1 + SSD_CONV_DIM
    d1 = x1 + SSD_HEADS
    w_z, w_xbc, w_dt, w_u = w_in[:, z0:z1], w_in[:, z1:x1], w_in[:, x1:d1], w_in[:, d1:]
    w_dtx = jnp.repeat(w_dt, SSD_HEAD_DIM, axis=1)
    w_all = jnp.concatenate([w_z, w_xbc, w_u, w_dtx], axis=1).astype(BF16)
    perm = jnp.concatenate([jnp.arange(0, SSD_HEADS, 2), jnp.arange(1, SSD_HEADS, 2)])
    a = -jnp.exp(a_log.astype(F32))
    return (
        w_all,
        w_dt.T[perm].astype(BF16),
        conv_w.astype(F32), conv_b.astype(F32)[None, :],
        rep(dt_bias), dt_bias.astype(F32)[perm][:, None],
        rep(a), a[perm][:, None],
        rep(d_skip), ssd_norm.astype(F32)[None, :],
        w_pool.astype(BF16), pool_scale.astype(F32)[None, :],
        w_out.astype(BF16), ln_g.astype(F32)[None, :], ln_b.astype(F32)[None, :],
    )


def _l1_kernel(x_ref, st0_ref, w_in_ref, lb_ref, normw_ref, w_out_ref, lng_ref, lnb_ref,
               y_ref, st_o_ref,
               q_buf, k_buf, lf_buf, v_buf, gt_buf, g_buf, s_ref, mix_buf,
               *, tl, l_total, l_valid):
    j = pl.program_id(1)
    nt = l_total // tl
    c = HGRN_C
    nc = tl // c
    nsub = c // HGRN_SUB
    masked = l_valid < l_total

    @pl.when(j == 0)
    def _():
        s_ref[...] = st0_ref[0]

    x = x_ref[0]
    xb = x.astype(BF16)
    lb = lb_ref[...]
    q_raw = _dot(xb, w_in_ref[:, 0:1024])
    fr = _dot(xb, w_in_ref[:, 1024:2048])
    v_buf[...] = _dot(xb, w_in_ref[:, 2048:3072])
    gt_buf[...] = _silu(_dot(xb, w_in_ref[:, 3072:4096]))

    log_sig = jnp.minimum(fr, 0.0) - jnp.log1p(jnp.exp(-jnp.abs(fr)))
    t1 = jnp.log(lb)
    t2 = jnp.log1p(-lb) + log_sig
    log_f = jnp.maximum(t1, t2) + jnp.log1p(jnp.exp(-jnp.abs(t1 - t2)))
    kk = (1.0 - lb) * jax.nn.sigmoid(-fr)
    if masked:
        ok = (j * tl + _iota((tl, 1), 0)) < l_valid
        log_f = jnp.where(ok, log_f, 0.0)
        kk = jnp.where(ok, kk, 0.0)
    lf_buf[...] = log_f
    k_buf[...] = kk
    q_buf[...] = _silu(q_raw)

    ltri = (_iota((c, c), 1) <= _iota((c, c), 0)).astype(F32)
    s_col = _iota((HGRN_SUB, c), 1)
    l_row = _iota((HGRN_SUB, c), 0)
    normw = normw_ref[...]

    def chunk(ci, carry):
        r0 = pl.multiple_of(ci * c, c)
        rows = pl.ds(r0, c)
        g_all = _dot(ltri, lf_buf[rows, :], HIGHEST)
        g_buf[...] = g_all
        g_last = g_all[c - 1:c, :]
        q_c = q_buf[rows, :]
        k_c = k_buf[rows, :]
        qt = q_c * jnp.exp(g_all)
        kd = k_c * jnp.exp(g_last - g_all)
        for h in range(HGRN_HEADS):
            hs = slice(h * HGRN_DK, (h + 1) * HGRN_DK)
            st = s_ref[h]
            v_h = v_buf[rows, hs].astype(BF16)
            o_h = _dot_nt(qt[:, hs].astype(BF16), st.astype(BF16))
            sc_rows = []
            for si in range(nsub):
                lo, hi = si * HGRN_SUB, (si + 1) * HGRN_SUB
                g_sub = g_buf[lo:hi, hs]
                if si == 0:
                    a_i = q_c[lo:hi, hs] * jnp.exp(g_sub)
                    b_i = k_c[0:hi, hs] * jnp.exp(-g_buf[0:hi, hs])
                else:
                    g_ref_row = g_buf[lo - 1:lo, hs]
                    a_i = q_c[lo:hi, hs] * jnp.exp(g_sub - g_ref_row)
                    b_i = k_c[0:hi, hs] * jnp.exp(g_ref_row - g_buf[0:hi, hs])
                b_i = b_i.astype(BF16)
                if hi < c:
                    b_i = jnp.concatenate([b_i, jnp.zeros((c - hi, HGRN_DK), BF16)], axis=0)
                sc = _dot_nt(a_i.astype(BF16), b_i)
                sc_rows.append(jnp.where(s_col <= l_row + lo, sc, 0.0))
            scores = jnp.concatenate(sc_rows, axis=0).astype(BF16)
            o_h = o_h + _dot(scores, v_h)
            s_ref[h] = st * jnp.exp(g_last[:, hs]) + _dot_tn(v_h, kd[:, hs].astype(BF16))
            ms = jnp.mean(o_h * o_h, axis=-1, keepdims=True)
            mix_buf[rows, hs] = (o_h * lax.rsqrt(ms + EPS) * normw * gt_buf[rows, hs]).astype(BF16)
        return carry

    lax.fori_loop(0, nc, chunk, 0)

    mix = _dot(mix_buf[...], w_out_ref[...])
    y_ref[0] = _layer_norm(ALPHA * x + mix, lng_ref[...], lnb_ref[...])

    @pl.when(j == nt - 1)
    def _():
        st_o_ref[0] = s_ref[...]


def _l1_mixer(x, st0_t, params, *, tl, l_valid):
    bsz, l_total, _ = x.shape
    nt = l_total // tl
    st_spec = pl.BlockSpec((1, HGRN_HEADS, HGRN_DV, HGRN_DK), lambda b, j: (b, 0, 0, 0))
    in_specs = [pl.BlockSpec((1, tl, D_MODEL), lambda b, j: (b, j, 0)), st_spec]
    in_specs += [_const_spec(p.shape) for p in params]
    big = lambda: pltpu.VMEM((tl, D_MODEL), F32)
    return pl.pallas_call(
        functools.partial(_l1_kernel, tl=tl, l_total=l_total, l_valid=l_valid),
        grid=(bsz, nt),
        in_specs=in_specs,
        out_specs=(pl.BlockSpec((1, tl, D_MODEL), lambda b, j: (b, j, 0)), st_spec),
        out_shape=(jax.ShapeDtypeStruct((bsz, l_total, D_MODEL), F32),
                   jax.ShapeDtypeStruct((bsz, HGRN_HEADS, HGRN_DV, HGRN_DK), F32)),
        scratch_shapes=[big(), big(), big(), big(), big(),
                        pltpu.VMEM((HGRN_C, D_MODEL), F32),
                        pltpu.VMEM((HGRN_HEADS, HGRN_DV, HGRN_DK), F32),
                        pltpu.VMEM((tl, D_MODEL), BF16)],
        compiler_params=pltpu.CompilerParams(
            dimension_semantics=("arbitrary", "arbitrary"), vmem_limit_bytes=VMEM_LIMIT),
        name="l1_mixer",
    )(x, st0_t, *params)


def _l1_params(w_in, lb, hgrn_norm, w_out, ln_g, ln_b):
    return (w_in.astype(BF16), lb.astype(F32)[None, :], hgrn_norm.astype(F32)[None, :],
            w_out.astype(BF16), ln_g.astype(F32)[None, :], ln_b.astype(F32)[None, :])


def _route_kernel(x_ref, wrt_ref, bias_ref, idx_ref, gate_ref, pos_ref, cnt_ref, count_sc, *, tt):
    i = pl.program_id(0)

    @pl.when(i == 0)
    def _():
        count_sc[...] = jnp.zeros_like(count_sc)

    neg = -jnp.inf
    scores = jax.nn.sigmoid(_dot_nt(wrt_ref[...], x_ref[...], HIGHEST))
    sel3 = (scores + bias_ref[...]).reshape(N_EXPERT_GROUPS, GROUP_SIZE, tt)

    im = _iota(sel3.shape, 1)
    m1 = jnp.max(sel3, axis=1, keepdims=True)
    i1 = jnp.min(jnp.where(sel3 == m1, im, GROUP_SIZE), axis=1, keepdims=True)
    m2 = jnp.max(jnp.where(im == i1, neg, sel3), axis=1, keepdims=True)
    grp = (m1 + m2).reshape(N_EXPERT_GROUPS, tt)

    ig = _iota(grp.shape, 0)
    keep = jnp.zeros(grp.shape, F32)
    for _ in range(TOPK_GROUPS):
        gm = jnp.max(grp, axis=0, keepdims=True)
        hit = ig == jnp.min(jnp.where(grp == gm, ig, N_EXPERT_GROUPS), axis=0, keepdims=True)
        keep = jnp.where(hit, 1.0, keep)
        grp = jnp.where(hit, neg, grp)

    selm = jnp.where(keep.reshape(N_EXPERT_GROUPS, 1, tt) > 0.0, sel3, neg).reshape(N_EXPERTS, tt)
    ie = _iota(selm.shape, 0)
    membf = jnp.zeros(selm.shape, F32)
    idx_rows, w_rows = [], []
    for _ in range(TOP_K):
        m = jnp.max(selm, axis=0, keepdims=True)
        ei = jnp.min(jnp.where(selm == m, ie, N_EXPERTS), axis=0, keepdims=True)
        hit = ie == ei
        idx_rows.append(ei)
        w_rows.append(jnp.sum(jnp.where(hit, scores, 0.0), axis=0, keepdims=True))
        selm = jnp.where(hit, neg, selm)
        membf = jnp.where(hit, 1.0, membf)
    idx = jnp.concatenate(idx_rows, axis=0)
    w = jnp.concatenate(w_rows, axis=0)
    gate_ref[...] = w / jnp.sum(w, axis=0, keepdims=True) * ROUTED_SCALE
    idx_ref[...] = idx

    before =(_iota((tt, tt), 0) < _iota((tt, tt), 1)).astype(BF16)
    pos_all = _dot(membf.astype(BF16), before) + count_sc[:, 0:1]
    pos_rows = [jnp.sum(jnp.where(ie == idx_rows[k], pos_all, 0.0), axis=0, keepdims=True)
                for k in range(TOP_K)]
    pos_ref[...] = jnp.concatenate(pos_rows, axis=0).astype(I32)
    count_sc[...] = count_sc[...] + jnp.sum(membf, axis=1, keepdims=True)
    cnt_ref[...] = count_sc[...].astype(I32)


def _route(x2, w_r, b_r, *, tt):
    t_total = x2.shape[0]
    row_spec = pl.BlockSpec((TOP_K, tt), lambda i: (0, i))
    return pl.pallas_call(
        functools.partial(_route_kernel, tt=tt),
        grid=(t_total // tt,),
        in_specs=[pl.BlockSpec((tt, D_MODEL), lambda i: (i, 0)),
                  _const_spec((N_EXPERTS, D_MODEL)), _const_spec((N_EXPERTS, 1))],
        out_specs=(row_spec, row_spec, row_spec, _const_spec((N_EXPERTS, LANES))),
        out_shape=(jax.ShapeDtypeStruct((TOP_K, t_total), I32),
                   jax.ShapeDtypeStruct((TOP_K, t_total), F32),
                   jax.ShapeDtypeStruct((TOP_K, t_total), I32),
                   jax.ShapeDtypeStruct((N_EXPERTS, LANES), I32)),
        scratch_shapes=[pltpu.VMEM((N_EXPERTS, LANES), F32)],
        compiler_params=pltpu.CompilerParams(dimension_semantics=("arbitrary",), vmem_limit_bytes=VMEM_LIMIT),
        name="moe_route",
    )(x2, w_r.astype(F32).T, b_r.astype(F32)[:, None])


def _dispatch_kernel(dest_ref, x_ref, xs_hbm, sem, *, tt):
    def row_copy(t, k):
        return pltpu.make_async_copy(x_ref.at[pl.ds(t, 1), :], xs_hbm.at[pl.ds(dest_ref[t * TOP_K + k], 1), :], sem)

    def issue(t, carry):
        for k in range(TOP_K):
            row_copy(t, k).start()
        return carry

    def drain(t, carry):
        for k in range(TOP_K):
            row_copy(t, k).wait()
        return carry

    lax.fori_loop(0, tt, issue, 0)
    lax.fori_loop(0, tt, drain, 0)


def _dispatch(x2, dest_flat, n_rows, *, tt):
    t_total = x2.shape[0]
    return pl.pallas_call(
        functools.partial(_dispatch_kernel, tt=tt),
        grid=(t_total // tt,),
        in_specs=[pl.BlockSpec((tt * TOP_K,), lambda i: (i,), memory_space=pltpu.SMEM),
                  pl.BlockSpec((tt, D_MODEL), lambda i: (i, 0))],
        out_specs=pl.BlockSpec(memory_space=pl.ANY),
        out_shape=jax.ShapeDtypeStruct((n_rows, D_MODEL), F32),
        scratch_shapes=[pltpu.SemaphoreType.DMA(())],
        compiler_params=pltpu.CompilerParams(dimension_semantics=("arbitrary",), vmem_limit_bytes=VMEM_LIMIT),
        name="moe_dispatch",
    )(dest_flat, x2)


def _experts_kernel(blk_e_ref, blk_rows_ref, n_used_ref, xs_ref, w_in_ref, w_out_ref, ys_ref, *, bm):
    i = pl.program_id(0)

    @pl.when(i < n_used_ref[0])
    def _():
        live = _iota((bm, 1), 0) < blk_rows_ref[i]
        xb = jnp.where(live, xs_ref[...], 0.0).astype(BF16)
        h = _dot(xb, w_in_ref[0])
        act = (_silu(h[:, 0:EXPERT_FF]) * h[:, EXPERT_FF:2 * EXPERT_FF]).astype(BF16)
        ys_ref[...] = _dot(act, w_out_ref[0])


def _experts(xs, blk_e, blk_rows, n_used, w_in, w_out, *, bm):
    n_rows = xs.shape[0]
    n_blocks = n_rows // bm
    last = lambda i, be, br, nu: jnp.minimum(i, nu[0] - 1)
    grid_spec = pltpu.PrefetchScalarGridSpec(
        num_scalar_prefetch=3,
        grid=(n_blocks,),
        in_specs=[pl.BlockSpec((bm, D_MODEL), lambda i, be, br, nu: (last(i, be, br, nu), 0)),
                  pl.BlockSpec((1, D_MODEL, 2 * EXPERT_FF), lambda i, be, br, nu: (be[last(i, be, br, nu)], 0, 0)),
                  pl.BlockSpec((1, EXPERT_FF, D_MODEL), lambda i, be, br, nu: (be[last(i, be, br, nu)], 0, 0))],
        out_specs=pl.BlockSpec((bm, D_MODEL), lambda i, be, br, nu: (last(i, be, br, nu), 0)),
    )
    return pl.pallas_call(
        functools.partial(_experts_kernel, bm=bm),
        grid_spec=grid_spec,
        out_shape=jax.ShapeDtypeStruct((n_rows, D_MODEL), F32),
        compiler_params=pltpu.CompilerParams(dimension_semantics=("arbitrary",), vmem_limit_bytes=VMEM_LIMIT),
        name="moe_experts",
    )(blk_e, blk_rows, n_used, xs, w_in, w_out)


def _combine_kernel(dest_ref, x_ref, gate_ref, ys_hbm, ws_in_ref, ws_out_ref, lng_ref, lnb_ref,
                    y_ref, buf, sem, *, tt):
    def row_copy(t, k):
        return pltpu.make_async_copy(ys_hbm.at[pl.ds(dest_ref[t * TOP_K + k], 1), :], buf.at[k, pl.ds(t, 1), :], sem)

    def issue(t, carry):
        for k in range(TOP_K):
            row_copy(t, k).start()
        return carry

    def drain(t, carry):
        for k in range(TOP_K):
            row_copy(t, k).wait()
        return carry

    lax.fori_loop(0, tt, issue, 0)

    x = x_ref[...]
    h = _dot(x.astype(BF16), ws_in_ref[...])
    act = (_silu(h[:, 0:EXPERT_FF]) * h[:, EXPERT_FF:2 * EXPERT_FF]).astype(BF16)
    shared = _dot(act, ws_out_ref[...])

    lax.fori_loop(0, tt, drain, 0)
    gates = gate_ref[...]
    routed = buf[0] * gates[:, 0:1]
    for k in range(1, TOP_K):
        routed = routed + buf[k] * gates[:, k:k + 1]
    y_ref[...] = _layer_norm(ALPHA * x + (routed + shared), lng_ref[...], lnb_ref[...])


def _combine(x2, dest_flat, gates, ys, ws_in, ws_out, ln_g, ln_b, *, tt):
    t_total = x2.shape[0]
    return pl.pallas_call(
        functools.partial(_combine_kernel, tt=tt),
        grid=(t_total // tt,),
        in_specs=[pl.BlockSpec((tt * TOP_K,), lambda i: (i,), memory_space=pltpu.SMEM),
                  pl.BlockSpec((tt, D_MODEL), lambda i: (i, 0)),
                  pl.BlockSpec((tt, TOP_K), lambda i: (i, 0)),
                  pl.BlockSpec(memory_space=pl.ANY),
                  _const_spec((D_MODEL, 2 * EXPERT_FF)), _const_spec((EXPERT_FF, D_MODEL)),
                  _const_spec((1, D_MODEL)), _const_spec((1, D_MODEL))],
        out_specs=pl.BlockSpec((tt, D_MODEL), lambda i: (i, 0)),
        out_shape=jax.ShapeDtypeStruct((t_total, D_MODEL), F32),
        scratch_shapes=[pltpu.VMEM((TOP_K, tt, D_MODEL), F32), pltpu.SemaphoreType.DMA(())],
        compiler_params=pltpu.CompilerParams(dimension_semantics=("arbitrary",), vmem_limit_bytes=VMEM_LIMIT),
        name="moe_combine",
    )(dest_flat, x2, gates, ys, ws_in.astype(BF16), ws_out.astype(BF16),
      ln_g.astype(F32)[None, :], ln_b.astype(F32)[None, :])


def _moe_ffn(x, w_r, b_r, w_in, w_out, ws_in, ws_out, ln_g, ln_b, *, tt, bm):
    bsz, l_total, _ = x.shape
    t_total = bsz * l_total
    x2 = x.reshape(t_total, D_MODEL)
    idx_t, gate_t, pos_t, cnt = _route(x2, w_r, b_r, tt=min(tt, t_total))

    counts = cnt[:, 0]
    padded = (counts + bm - 1) // bm * bm
    pad_end = jnp.cumsum(padded)
    pad_start = pad_end - padded
    onehot = idx_t[:, :, None] == jnp.arange(N_EXPERTS, dtype=I32)
    dest_t = pos_t + jnp.sum(jnp.where(onehot, pad_start, 0), axis=-1)
    dest_flat = dest_t.T.reshape(-1)
    n_blocks = -(-(t_total * TOP_K + N_EXPERTS * (bm - 1)) // bm)
    blk_start = jnp.arange(n_blocks, dtype=I32) * bm
    blk_e = jnp.minimum(jnp.searchsorted(pad_end, blk_start, side="right"), N_EXPERTS - 1).astype(I32)
    blk_rows = jnp.clip(pad_start[blk_e] + counts[blk_e] - blk_start, 0, bm).astype(I32)
    n_used = (pad_end[-1:] // bm).astype(I32)

    tt_d = min(tt, t_total)
    xs = _dispatch(x2, dest_flat, n_blocks * bm, tt=tt_d)
    ys = _experts(xs, blk_e, blk_rows, n_used, w_in.astype(BF16), w_out.astype(BF16), bm=bm)
    y = _combine(x2, dest_flat, gate_t.T, ys, ws_in, ws_out, ln_g, ln_b, tt=tt_d)
    return y.reshape(bsz, l_total, D_MODEL)


def _pad_time(t, total):
    return jnp.pad(t, [(0, 0), (0, total - t.shape[1]), (0, 0)])


def _trunk(x, pos0, conv_st, ssm_st, pool_st, hgrn_st, weights, *, tl, tt, bm):
    (w_in_l0, conv_w_l0, conv_b_l0, dt_bias_l0, a_log_l0, d_skip_l0, ssd_norm_l0, w_pool_l0,
     pool_scale_l0, w_out_l0, w_in_l1, lb_raw, hgrn_norm_l1, w_out_l1, ln_mix_g, ln_mix_b,
     ln_ffn_g, ln_ffn_b, router_w, router_bias, moe_w_in, moe_w_out, shared_w_in, shared_w_out) = weights
    bsz, l_valid, _ = x.shape
    l_total = -(-l_valid // tl) * tl
    xp = _pad_time(x, l_total)

    lb_sm = jax.nn.softmax(lb_raw.astype(F32), axis=0)
    lb_all = jnp.cumsum(lb_sm, axis=0) - lb_sm[0]

    conv0 = jnp.pad(conv_st.astype(F32), [(0, 0), (8 - (SSD_CONV_W - 1), 0), (0, 0)])
    pool0 = jnp.pad(pool_st.astype(F32), [(0, 0), (1, 0), (0, 0)])
    ssm0_t = ssm_st.astype(F32).reshape(bsz, SSD_D_INNER, SSD_STATE).transpose(0, 2, 1)
    p0 = _l0_params(w_in_l0, conv_w_l0, conv_b_l0, dt_bias_l0, a_log_l0, d_skip_l0, ssd_norm_l0,
                    w_pool_l0, pool_scale_l0, w_out_l0, ln_mix_g[0], ln_mix_b[0])
    h, conv_o, ssm_o, pool_o = _l0_mixer(xp, conv0, ssm0_t, pool0, p0, tl=tl, l_valid=l_valid, pos0=pos0)
    new_conv = conv_o[:, 8 - (SSD_CONV_W - 1):]
    new_pool = pool_o[:, 1:]
    new_ssm = ssm_o.transpose(0, 2, 1).reshape(bsz, SSD_HEADS, SSD_HEAD_DIM, SSD_STATE)
    h = _moe_ffn(h[:, :l_valid], router_w[0], router_bias[0], moe_w_in[0], moe_w_out[0], shared_w_in[0],
                 shared_w_out[0], ln_ffn_g[0], ln_ffn_b[0], tt=tt, bm=bm)

    p1 = _l1_params(w_in_l1, lb_all[1], hgrn_norm_l1, w_out_l1, ln_mix_g[1], ln_mix_b[1])
    h, hgrn_o = _l1_mixer(_pad_time(h, l_total), hgrn_st.astype(F32).transpose(0, 1, 3, 2), p1,
                          tl=tl, l_valid=l_valid)
    new_hgrn = hgrn_o.transpose(0, 1, 3, 2)
    h = _moe_ffn(h[:, :l_valid], router_w[1], router_bias[1], moe_w_in[1], moe_w_out[1], shared_w_in[1],
                 shared_w_out[1], ln_ffn_g[1], ln_ffn_b[1], tt=tt, bm=bm)
    return h, new_conv, new_ssm, new_pool, new_hgrn


PROMPT_TL = 256
SAMPLE_TL = 128
TOKEN_TILE = 256
EXPERT_BLOCK = 256


def kernel(x_prompt, x_sample, state_conv_l0, state_ssm_l0, state_pool_l0, state_hgrn_l1, w_in_l0, conv_w_l0, conv_b_l0, dt_bias_l0, a_log_l0, d_skip_l0, ssd_norm_l0, w_pool_l0, pool_scale_l0, w_out_l0, w_in_l1, lb_raw, hgrn_norm_l1, w_out_l1, ln_mix_g, ln_mix_b, ln_ffn_g, ln_ffn_b, router_w, router_bias, moe_w_in, moe_w_out, shared_w_in, shared_w_out):
    weights = (w_in_l0, conv_w_l0, conv_b_l0, dt_bias_l0, a_log_l0, d_skip_l0, ssd_norm_l0, w_pool_l0,
               pool_scale_l0, w_out_l0, w_in_l1, lb_raw, hgrn_norm_l1, w_out_l1, ln_mix_g, ln_mix_b,
               ln_ffn_g, ln_ffn_b, router_w, router_bias, moe_w_in, moe_w_out, shared_w_in, shared_w_out)
    bp = x_prompt.shape[0]
    past_len = 1024
    zeros = lambda *s: jnp.zeros(s, F32)
    y_p, p_conv, p_ssm, p_pool, p_hgrn = _trunk(
        x_prompt, 0, zeros(bp, SSD_CONV_W - 1, SSD_CONV_DIM), zeros(bp, SSD_HEADS, SSD_HEAD_DIM, SSD_STATE),
        zeros(bp, POOL_MAXW - 1, POOL_DIM), zeros(bp, HGRN_HEADS, HGRN_DK, HGRN_DV), weights,
        tl=PROMPT_TL, tt=TOKEN_TILE, bm=EXPERT_BLOCK)
    y_s, s_conv, s_ssm, s_pool, s_hgrn = _trunk(
        x_sample, past_len, state_conv_l0, state_ssm_l0, state_pool_l0, state_hgrn_l1, weights,
        tl=SAMPLE_TL, tt=TOKEN_TILE, bm=EXPERT_BLOCK)
    return (y_p, y_s, p_conv, p_ssm, p_pool, p_hgrn, s_conv, s_ssm, s_pool, s_hgrn)
```

```python
import functools
import math

import jax
import jax.numpy as jnp
from jax import lax
from jax.experimental import pallas as pl
from jax.experimental.pallas import tpu as pltpu

F32 = jnp.float32
BF16 = jnp.bfloat16
I32 = jnp.int32
HIGHEST = lax.Precision.HIGHEST

D_MODEL = 1024
DEPTH = 2
ALPHA = (2 * DEPTH) ** 0.25
EPS = 1e-5

SSD_HEADS = 16
SSD_HEAD_DIM = 64
SSD_GROUPS = 2
SSD_STATE = 128
SSD_D_INNER = 1024
SSD_CONV_DIM = 1536
SSD_CONV_W = 4
SSD_Q = 64
GROUP_CH = SSD_D_INNER // SSD_GROUPS

POOL_DIM = 1024
POOL_WINDOWS = (2, 4, 8, 16)
POOL_GROUP_DIM = 256
POOL_MAXW = 16

HGRN_HEADS = 8
HGRN_DK = 128
HGRN_DV = 128
HGRN_C = 128
HGRN_SUB = 16

N_EXPERTS = 64
N_EXPERT_GROUPS = 8
GROUP_SIZE = N_EXPERTS // N_EXPERT_GROUPS
TOPK_GROUPS = 4
TOP_K = 8
EXPERT_FF = 256
ROUTED_SCALE = 2.5
ROUTE_TILE = 512

V7X_VMEM_BYTES = 64 * 1024 * 1024
VMEM_LIMIT = V7X_VMEM_BYTES - 8 * 1024 * 1024
LANES = 128
SUBLANES = 8
assert SUBLANES * LANES == D_MODEL

NT_DIMS = (((1,), (1,)), ((), ()))
TN_DIMS = (((0,), (0,)), ((), ()))


def _dot(a, b, precision=None):
    return jnp.dot(a, b, preferred_element_type=F32, precision=precision)


def _dot_nt(a, b, precision=None):
    return lax.dot_general(a, b, NT_DIMS, preferred_element_type=F32, precision=precision)


def _dot_tn(a, b):
    return lax.dot_general(a, b, TN_DIMS, preferred_element_type=F32)


def _softplus(v):
    return jnp.maximum(v, 0.0) + jnp.log1p(jnp.exp(-jnp.abs(v)))


def _silu(v):
    return v * jax.nn.sigmoid(v)


def _layer_norm(h, g, b):
    mu = jnp.mean(h, axis=-1, keepdims=True)
    hc = h - mu
    var = jnp.mean(hc * hc, axis=-1, keepdims=True)
    return hc * lax.rsqrt(var + EPS) * g + b


def _iota(shape, dim):
    return lax.broadcasted_iota(I32, shape, dim)


def _l0_kernel(x_ref, conv0_ref, ssm0_ref, pool0_ref,
               w_in_ref, w_dtt_ref, conv_w_ref, conv_b_ref, dtb_x_ref, dtb_c_ref, a_x_ref, a_c_ref,
               dskip_ref, normw_ref, w_pool_ref, pscale_ref, w_out_ref, lng_ref, lnb_ref,
               y_ref, conv_o_ref, ssm_o_ref, pool_o_ref,
               xbc_buf, u_buf, xa_buf, z_buf, dtx_buf, h_ref, mix_buf,
               *, tl, l_total, l_valid, pos0):
    j = pl.program_id(1)
    nt = l_total // tl
    q = SSD_Q
    nq = tl // q
    masked = l_valid < l_total

    @pl.when(j == 0)
    def _():
        xbc_buf[0:8, :] = conv0_ref[0]
        u_buf[0:16, :] = pool0_ref[0]
        h_ref[...] = ssm0_ref[0]

    x = x_ref[0]
    xb = x.astype(BF16)

    z_buf[...] = _dot(xb, w_in_ref[:, 0:1024])
    xbc_buf[8:8 + tl, :] = _dot(xb, w_in_ref[:, 1024:2560])
    u_buf[16:16 + tl, :] = _dot(xb, w_in_ref[:, 2560:3584])
    dtx = _softplus(_dot(xb, w_in_ref[:, 3584:4608]) + dtb_x_ref[...])
    if masked:
        t_col = j * tl + _iota((tl, 1), 0)
        dtx = jnp.where(t_col < l_valid, dtx, 0.0)
    dtx_buf[...] = dtx

    acc = conv_b_ref[...] + conv_w_ref[3:4, :] * xbc_buf[8:8 + tl, :]
    for jj in range(SSD_CONV_W - 1):
        acc = acc + conv_w_ref[jj:jj + 1, :] * xbc_buf[5 + jj:5 + jj + tl, :]
    xa_buf[...] = _silu(acc)

    a_x = a_x_ref[...]
    a_c = a_c_ref[...]
    dskip = dskip_ref[...]
    normw = normw_ref[...]

    ltri = (_iota((q, q), 1) <= _iota((q, q), 0)).astype(F32)
    sp = _iota((q, LANES), 0)
    ln = _iota((q, LANES), 1)
    mt_left = ((ln < q) & (sp <= ln)).astype(F32)
    mt_right = ((ln >= q) & (sp <= ln - q)).astype(F32)
    causal2 = (ln % q) <= sp
    low_half = ln < q

    def chunk(c, carry):
        r0 = pl.multiple_of(c * q, q)
        rows = pl.ds(r0, q)
        xs = xa_buf[rows, 0:1024]
        dtc = dtx_buf[rows, :]
        cum = _dot(ltri, dtc * a_x, HIGHEST)
        ecum = jnp.exp(cum)
        cum_last = cum[q - 1:q, :]
        xw = xs * (jnp.exp(cum_last - cum) * dtc)
        xdt = xs * dtc

        xcb = x_ref[0, rows, :].astype(BF16)
        dtt = _softplus(_dot_nt(w_dtt_ref[...], xcb) + dtb_c_ref[...])
        if masked:
            t_row = j * tl + r0 + _iota((1, q), 1)
            dtt = jnp.where(t_row < l_valid, dtt, 0.0)
        dat = dtt * a_c
        cum_t2 = _dot(dat[0:8, :], mt_left, HIGHEST) + _dot(dat[8:16, :], mt_right, HIGHEST)

        pieces = []
        for g in range(SSD_GROUPS):
            gs = slice(g * GROUP_CH, (g + 1) * GROUP_CH)
            bg = xa_buf[rows, 1024 + g * SSD_STATE:1024 + (g + 1) * SSD_STATE].astype(BF16)
            cg = xa_buf[rows, 1280 + g * SSD_STATE:1280 + (g + 1) * SSD_STATE].astype(BF16)
            cb2 = _dot_nt(cg, jnp.concatenate([bg, bg], axis=0))
            h_g = h_ref[:, gs]
            y_off = _dot(cg, h_g.astype(BF16)) * ecum[:, gs]
            for pi in range(GROUP_CH // LANES):
                i = g * (GROUP_CH // LANES) + pi
                ls = slice(i * LANES, (i + 1) * LANES)
                seg = cum[:, ls] - cum_t2[i:i + 1, :]
                mixp = (cb2 * jnp.exp(jnp.where(causal2, seg, -jnp.inf))).astype(BF16)
                xp = xdt[:, ls]
                wblk = jnp.concatenate([jnp.where(low_half, xp, 0.0), jnp.where(low_half, 0.0, xp)],
                                       axis=0).astype(BF16)
                y_pair = (_dot(mixp, wblk) + y_off[:, pi * LANES:(pi + 1) * LANES]
                          + xs[:, ls] * dskip[:, ls])
                pieces.append(y_pair)
            h_ref[:, gs] = h_g * ecum[q - 1:q, gs] + _dot_tn(bg, xw[:, gs].astype(BF16))

        for g in range(SSD_GROUPS):
            gs = slice(g * GROUP_CH, (g + 1) * GROUP_CH)
            yg = jnp.concatenate(pieces[g * 4:(g + 1) * 4], axis=1) * _silu(z_buf[rows, gs])
            ms = jnp.mean(yg * yg, axis=-1, keepdims=True)
            mix_buf[rows, gs] = (yg * lax.rsqrt(ms + EPS) * normw[:, gs]).astype(BF16)
        return carry

    lax.fori_loop(0, nq, chunk, 0, unroll=True)

    pos = (pos0 + j * tl + _iota((tl, 1), 0)).astype(F32)
    for gi, w in enumerate(POOL_WINDOWS):
        cs = slice(gi * POOL_GROUP_DIM, (gi + 1) * POOL_GROUP_DIM)
        cur = u_buf[:, cs]
        d = 1
        while d < w:
            cur = cur + pltpu.roll(cur, d, 0)
            d *= 2
        cnt = jnp.minimum(pos + 1.0, float(w))
        diff = cur[16:, :] / cnt - u_buf[16:16 + tl, cs]
        yb = _dot(diff.astype(BF16), w_pool_ref[gi]) * pscale_ref[:, cs]
        mix_buf[:, SSD_D_INNER + gi * POOL_GROUP_DIM:SSD_D_INNER + (gi + 1) * POOL_GROUP_DIM] = yb.astype(BF16)

    mix = _dot(mix_buf[...], w_out_ref[...])
    y_ref[0] = _layer_norm(ALPHA * x + mix, lng_ref[...], lnb_ref[...])

    lv = l_valid - (nt - 1) * tl

    @pl.when(j == nt - 1)
    def _():
        conv_o_ref[0] = xbc_buf[lv:lv + 8, :]
        pool_o_ref[0] = u_buf[lv:lv + 16, :]
        ssm_o_ref[0] = h_ref[...]

    @pl.when(j < nt - 1)
    def _():
        xbc_buf[0:8, :] = xbc_buf[tl:tl + 8, :]
        u_buf[0:16, :] = u_buf[tl:tl + 16, :]


def _const_spec(shape):
    nd = len(shape)
    return pl.BlockSpec(shape, lambda *_: (0,) * nd)


def _l0_mixer(x, conv0, ssm0_t, pool0, params, *, tl, l_valid, pos0):
    bsz, l_total, _ = x.shape
    nt = l_total // tl
    per_b3 = lambda b, j: (b, 0, 0)
    in_specs = [
        pl.BlockSpec((1, tl, D_MODEL), lambda b, j: (b, j, 0)),
        pl.BlockSpec((1, 8, SSD_CONV_DIM), per_b3),
        pl.BlockSpec((1, SSD_STATE, SSD_D_INNER), per_b3),
        pl.BlockSpec((1, 16, POOL_DIM), per_b3),
    ] + [_const_spec(p.shape) for p in params]
    out_shape = (
        jax.ShapeDtypeStruct((bsz, l_total, D_MODEL), F32),
        jax.ShapeDtypeStruct((bsz, 8, SSD_CONV_DIM), F32),
        jax.ShapeDtypeStruct((bsz, SSD_STATE, SSD_D_INNER), F32),
        jax.ShapeDtypeStruct((bsz, 16, POOL_DIM), F32),
    )
    out_specs = (
        pl.BlockSpec((1, tl, D_MODEL), lambda b, j: (b, j, 0)),
        pl.BlockSpec((1, 8, SSD_CONV_DIM), per_b3),
        pl.BlockSpec((1, SSD_STATE, SSD_D_INNER), per_b3),
        pl.BlockSpec((1, 16, POOL_DIM), per_b3),
    )
    scratch = [
        pltpu.VMEM((8 + tl, SSD_CONV_DIM), F32),
        pltpu.VMEM((16 + tl, POOL_DIM), F32),
        pltpu.VMEM((tl, SSD_CONV_DIM), F32),
        pltpu.VMEM((tl, SSD_D_INNER), F32),
        pltpu.VMEM((tl, SSD_D_INNER), F32),
        pltpu.VMEM((SSD_STATE, SSD_D_INNER), F32),
        pltpu.VMEM((tl, 2 * D_MODEL), BF16),
    ]
    return pl.pallas_call(
        functools.partial(_l0_kernel, tl=tl, l_total=l_total, l_valid=l_valid, pos0=pos0),
        grid=(bsz, nt),
        in_specs=in_specs,
        out_specs=out_specs,
        out_shape=out_shape,
        scratch_shapes=scratch,
        compiler_params=pltpu.CompilerParams(
            dimension_semantics=("arbitrary", "arbitrary"), vmem_limit_bytes=VMEM_LIMIT),
        name="l0_mixer",
    )(x, conv0, ssm0_t, pool0, *params)


def _l0_params(w_in, conv_w, conv_b, dt_bias, a_log, d_skip, ssd_norm, w_pool, pool_scale, w_out, ln_g, ln_b):
    rep = lambda v: jnp.repeat(v.astype(F32), SSD_HEAD_DIM)[None, :]
    z0, z1 = 0, SSD_D_INNER
    x1 = z1 + SSD_CONV_DIM
    d1 = x1 + SSD_HEADS
    w_z, w_xbc, w_dt, w_u = w_in[:, z0:z1], w_in[:, z1:x1], w_in[:, x1:d1], w_in[:, d1:]
    w_dtx = jnp.repeat(w_dt, SSD_HEAD_DIM, axis=1)
    w_all = jnp.concatenate([w_z, w_xbc, w_u, w_dtx], axis=1).astype(BF16)
    perm = jnp.concatenate([jnp.arange(0, SSD_HEADS, 2), jnp.arange(1, SSD_HEADS, 2)])
    a = -jnp.exp(a_log.astype(F32))
    return (
        w_all,
        w_dt.T[perm].astype(BF16),
        conv_w.astype(F32), conv_b.astype(F32)[None, :],
        rep(dt_bias), dt_bias.astype(F32)[perm][:, None],
        rep(a), a[perm][:, None],
        rep(d_skip), ssd_norm.astype(F32)[None, :],
        w_pool.astype(BF16), pool_scale.astype(F32)[None, :],
        w_out.astype(BF16), ln_g.astype(F32)[None, :], ln_b.astype(F32)[None, :],
    )


def _l1_kernel(x_ref, st0_ref, w_in_ref, lb_ref, normw_ref, w_out_ref, lng_ref, lnb_ref,
               y_ref, st_o_ref,
               q_buf, k_buf, lf_buf, v_buf, gt_buf, g_buf, s_ref, mix_buf,
               *, tl, l_total, l_valid):
    j = pl.program_id(1)
    nt = l_total // tl
    c = HGRN_C
    nc = tl // c
    nsub = c // HGRN_SUB
    masked = l_valid < l_total

    @pl.when(j == 0)
    def _():
        s_ref[...] = st0_ref[0]

    x = x_ref[0]
    xb = x.astype(BF16)
    lb = lb_ref[...]
    q_raw = _dot(xb, w_in_ref[:, 0:1024])
    fr = _dot(xb, w_in_ref[:, 1024:2048])
    v_buf[...] = _dot(xb, w_in_ref[:, 2048:3072])
    gt_buf[...] = _silu(_dot(xb, w_in_ref[:, 3072:4096]))

    log_sig = jnp.minimum(fr, 0.0) - jnp.log1p(jnp.exp(-jnp.abs(fr)))
    t1 = jnp.log(lb)
    t2 = jnp.log1p(-lb) + log_sig
    log_f = jnp.maximum(t1, t2) + jnp.log1p(jnp.exp(-jnp.abs(t1 - t2)))
    kk = (1.0 - lb) * jax.nn.sigmoid(-fr)
    if masked:
        ok = (j * tl + _iota((tl, 1), 0)) < l_valid
        log_f = jnp.where(ok, log_f, 0.0)
        kk = jnp.where(ok, kk, 0.0)
    lf_buf[...] = log_f
    k_buf[...] = kk
    q_buf[...] = _silu(q_raw)

    ltri = (_iota((c, c), 1) <= _iota((c, c), 0)).astype(F32)
    s_col = _iota((HGRN_SUB, c), 1)
    l_row = _iota((HGRN_SUB, c), 0)
    normw = normw_ref[...]

    def chunk(ci, carry):
        r0 = pl.multiple_of(ci * c, c)
        rows = pl.ds(r0, c)
        g_all = _dot(ltri, lf_buf[rows, :], HIGHEST)
        g_buf[...] = g_all
        g_last = g_all[c - 1:c, :]
        q_c = q_buf[rows, :]
        k_c = k_buf[rows, :]
        qt = q_c * jnp.exp(g_all)
        kd = k_c * jnp.exp(g_last - g_all)
        for h in range(HGRN_HEADS):
            hs = slice(h * HGRN_DK, (h + 1) * HGRN_DK)
            st = s_ref[h]
            v_h = v_buf[rows, hs].astype(BF16)
            o_h = _dot_nt(qt[:, hs].astype(BF16), st.astype(BF16))
            sc_rows = []
            for si in range(nsub):
                lo, hi = si * HGRN_SUB, (si + 1) * HGRN_SUB
                g_sub = g_buf[lo:hi, hs]
                if si == 0:
                    a_i = q_c[lo:hi, hs] * jnp.exp(g_sub)
                    b_i = k_c[0:hi, hs] * jnp.exp(-g_buf[0:hi, hs])
                else:
                    g_ref_row = g_buf[lo - 1:lo, hs]
                    a_i = q_c[lo:hi, hs] * jnp.exp(g_sub - g_ref_row)
                    b_i = k_c[0:hi, hs] * jnp.exp(g_ref_row - g_buf[0:hi, hs])
                b_i = b_i.astype(BF16)
                if hi < c:
                    b_i = jnp.concatenate([b_i, jnp.zeros((c - hi, HGRN_DK), BF16)], axis=0)
                sc = _dot_nt(a_i.astype(BF16), b_i)
                sc_rows.append(jnp.where(s_col <= l_row + lo, sc, 0.0))
            scores = jnp.concatenate(sc_rows, axis=0).astype(BF16)
            o_h = o_h + _dot(scores, v_h)
            s_ref[h] = st * jnp.exp(g_last[:, hs]) + _dot_tn(v_h, kd[:, hs].astype(BF16))
            ms = jnp.mean(o_h * o_h, axis=-1, keepdims=True)
            mix_buf[rows, hs] = (o_h * lax.rsqrt(ms + EPS) * normw * gt_buf[rows, hs]).astype(BF16)
        return carry

    lax.fori_loop(0, nc, chunk, 0, unroll=True)

    mix = _dot(mix_buf[...], w_out_ref[...])
    y_ref[0] = _layer_norm(ALPHA * x + mix, lng_ref[...], lnb_ref[...])

    @pl.when(j == nt - 1)
    def _():
        st_o_ref[0] = s_ref[...]


def _l1_mixer(x, st0_t, params, *, tl, l_valid):
    bsz, l_total, _ = x.shape
    nt = l_total // tl
    st_spec = pl.BlockSpec((1, HGRN_HEADS, HGRN_DV, HGRN_DK), lambda b, j: (b, 0, 0, 0))
    in_specs = [pl.BlockSpec((1, tl, D_MODEL), lambda b, j: (b, j, 0)), st_spec]
    in_specs += [_const_spec(p.shape) for p in params]
    big = lambda: pltpu.VMEM((tl, D_MODEL), F32)
    return pl.pallas_call(
        functools.partial(_l1_kernel, tl=tl, l_total=l_total, l_valid=l_valid),
        grid=(bsz, nt),
        in_specs=in_specs,
        out_specs=(pl.BlockSpec((1, tl, D_MODEL), lambda b, j: (b, j, 0)), st_spec),
        out_shape=(jax.ShapeDtypeStruct((bsz, l_total, D_MODEL), F32),
                   jax.ShapeDtypeStruct((bsz, HGRN_HEADS, HGRN_DV, HGRN_DK), F32)),
        scratch_shapes=[big(), big(), big(), big(), big(),
                        pltpu.VMEM((HGRN_C, D_MODEL), F32),
                        pltpu.VMEM((HGRN_HEADS, HGRN_DV, HGRN_DK), F32),
                        pltpu.VMEM((tl, D_MODEL), BF16)],
        compiler_params=pltpu.CompilerParams(
            dimension_semantics=("arbitrary", "arbitrary"), vmem_limit_bytes=VMEM_LIMIT),
        name="l1_mixer",
    )(x, st0_t, *params)


def _l1_params(w_in, lb, hgrn_norm, w_out, ln_g, ln_b):
    return (w_in.astype(BF16), lb.astype(F32)[None, :], hgrn_norm.astype(F32)[None, :],
            w_out.astype(BF16), ln_g.astype(F32)[None, :], ln_b.astype(F32)[None, :])


def _route_kernel(x_ref, wrt_ref, bias_ref, idx_ref, gate_ref, pos_ref, cnt_ref, count_sc, *, tt):
    i = pl.program_id(0)

    @pl.when(i == 0)
    def _():
        count_sc[...] = jnp.zeros_like(count_sc)

    neg = -jnp.inf
    scores = jax.nn.sigmoid(_dot_nt(wrt_ref[...], x_ref[...], HIGHEST))
    sel3 = (scores + bias_ref[...]).reshape(N_EXPERT_GROUPS, GROUP_SIZE, tt)

    im = _iota(sel3.shape, 1)
    m1 = jnp.max(sel3, axis=1, keepdims=True)
    i1 = jnp.min(jnp.where(sel3 == m1, im, GROUP_SIZE), axis=1, keepdims=True)
    m2 = jnp.max(jnp.where(im == i1, neg, sel3), axis=1, keepdims=True)
    grp = (m1 + m2).reshape(N_EXPERT_GROUPS, tt)

    ig = _iota(grp.shape, 0)
    keep = jnp.zeros(grp.shape, F32)
    for _ in range(TOPK_GROUPS):
        gm = jnp.max(grp, axis=0, keepdims=True)
        hit = ig == jnp.min(jnp.where(grp == gm, ig, N_EXPERT_GROUPS), axis=0, keepdims=True)
        keep = jnp.where(hit, 1.0, keep)
        grp = jnp.where(hit, neg, grp)

    selm = jnp.where(keep.reshape(N_EXPERT_GROUPS, 1, tt) > 0.0, sel3, neg).reshape(N_EXPERTS, tt)
    ie = _iota(selm.shape, 0)
    membf = jnp.zeros(selm.shape, F32)
    idx_rows, w_rows = [], []
    for _ in range(TOP_K):
        m = jnp.max(selm, axis=0, keepdims=True)
        ei = jnp.min(jnp.where(selm == m, ie, N_EXPERTS), axis=0, keepdims=True)
        hit = ie == ei
        idx_rows.append(ei)
        w_rows.append(jnp.sum(jnp.where(hit, scores, 0.0), axis=0, keepdims=True))
        selm = jnp.where(hit, neg, selm)
        membf = jnp.where(hit, 1.0, membf)
    idx = jnp.concatenate(idx_rows, axis=0)
    w = jnp.concatenate(w_rows, axis=0)
    gate_ref[...] = w / jnp.sum(w, axis=0, keepdims=True) * ROUTED_SCALE
    idx_ref[...] = idx

    before =(_iota((tt, tt), 0) < _iota((tt, tt), 1)).astype(BF16)
    pos_all = _dot(membf.astype(BF16), before) + count_sc[:, 0:1]
    pos_rows = [jnp.sum(jnp.where(ie == idx_rows[k], pos_all, 0.0), axis=0, keepdims=True)
                for k in range(TOP_K)]
    pos_ref[...] = jnp.concatenate(pos_rows, axis=0).astype(I32)
    count_sc[...] = count_sc[...] + jnp.sum(membf, axis=1, keepdims=True)
    cnt_ref[...] = count_sc[...].astype(I32)


def _route(x2, w_r, b_r, *, tt):
    t_total = x2.shape[0]
    row_spec = pl.BlockSpec((TOP_K, tt), lambda i: (0, i))
    return pl.pallas_call(
        functools.partial(_route_kernel, tt=tt),
        grid=(t_total // tt,),
        in_specs=[pl.BlockSpec((tt, D_MODEL), lambda i: (i, 0)),
                  _const_spec((N_EXPERTS, D_MODEL)), _const_spec((N_EXPERTS, 1))],
        out_specs=(row_spec, row_spec, row_spec, _const_spec((N_EXPERTS, LANES))),
        out_shape=(jax.ShapeDtypeStruct((TOP_K, t_total), I32),
                   jax.ShapeDtypeStruct((TOP_K, t_total), F32),
                   jax.ShapeDtypeStruct((TOP_K, t_total), I32),
                   jax.ShapeDtypeStruct((N_EXPERTS, LANES), I32)),
        scratch_shapes=[pltpu.VMEM((N_EXPERTS, LANES), F32)],
        compiler_params=pltpu.CompilerParams(dimension_semantics=("arbitrary",), vmem_limit_bytes=VMEM_LIMIT),
        name="moe_route",
    )(x2, w_r.astype(F32).T, b_r.astype(F32)[:, None])


def _rows_to_tiles(dst_ref, val):
    n = val.shape[0]
    for jj in range(SUBLANES):
        dst_ref[pl.ds(jj, n, stride=SUBLANES), :] = val[:, jj * LANES:(jj + 1) * LANES]


def _tiles_to_rows(src_ref):
    n = src_ref.shape[0] // SUBLANES
    return jnp.concatenate([src_ref[pl.ds(jj, n, stride=SUBLANES), :] for jj in range(SUBLANES)], axis=1)


def _tile_of_row(r):
    return pl.ds(pl.multiple_of(r * SUBLANES, SUBLANES), SUBLANES)


def _dispatch_kernel(dest_ref, x_ref, xs_hbm, x3, sem, *, tt):
    _rows_to_tiles(x3, x_ref[...])

    def issue(t, carry):
        for k in range(TOP_K):
            pltpu.make_async_copy(x3.at[_tile_of_row(t)], xs_hbm.at[_tile_of_row(dest_ref[t * TOP_K + k])],
                                  sem).start(priority=k % 2)
        return carry

    lax.fori_loop(0, tt, issue, 0)
    for _ in range(TOP_K):
        pltpu.make_async_copy(x3, xs_hbm.at[pl.ds(0, tt * SUBLANES)], sem).wait()


def _dispatch(x2, dest_flat, n_rows, *, tt):
    t_total = x2.shape[0]
    return pl.pallas_call(
        functools.partial(_dispatch_kernel, tt=tt),
        grid=(t_total // tt,),
        in_specs=[pl.BlockSpec((tt * TOP_K,), lambda i: (i,), memory_space=pltpu.SMEM),
                  pl.BlockSpec((tt, D_MODEL), lambda i: (i, 0))],
        out_specs=pl.BlockSpec(memory_space=pl.ANY),
        out_shape=jax.ShapeDtypeStruct((n_rows * SUBLANES, LANES), F32),
        scratch_shapes=[pltpu.VMEM((tt * SUBLANES, LANES), F32), pltpu.SemaphoreType.DMA(())],
        compiler_params=pltpu.CompilerParams(dimension_semantics=("arbitrary",), vmem_limit_bytes=VMEM_LIMIT),
        name="moe_dispatch",
    )(dest_flat, x2)


def _experts_kernel(blk_e_ref, blk_rows_ref, n_used_ref, xs_ref, w_in_ref, w_out_ref, ys_ref, *, bm):
    i = pl.program_id(0)

    @pl.when(i < n_used_ref[0])
    def _():
        live = _iota((bm, 1), 0) < blk_rows_ref[i]
        xb = jnp.where(live, _tiles_to_rows(xs_ref), 0.0).astype(BF16)
        h = _dot(xb, w_in_ref[0])
        act = (_silu(h[:, 0:EXPERT_FF]) * h[:, EXPERT_FF:2 * EXPERT_FF]).astype(BF16)
        _rows_to_tiles(ys_ref, _dot(act, w_out_ref[0]))


def _experts(xs, blk_e, blk_rows, n_used, w_in, w_out, *, bm):
    n_rows = xs.shape[0] // SUBLANES
    n_blocks = n_rows // bm
    last = lambda i, be, br, nu: jnp.minimum(i, nu[0] - 1)
    row_spec = pl.BlockSpec((bm * SUBLANES, LANES), lambda i, be, br, nu: (last(i, be, br, nu), 0))
    grid_spec = pltpu.PrefetchScalarGridSpec(
        num_scalar_prefetch=3,
        grid=(n_blocks,),
        in_specs=[row_spec,
                  pl.BlockSpec((1, D_MODEL, 2 * EXPERT_FF), lambda i, be, br, nu: (be[last(i, be, br, nu)], 0, 0)),
                  pl.BlockSpec((1, EXPERT_FF, D_MODEL), lambda i, be, br, nu: (be[last(i, be, br, nu)], 0, 0))],
        out_specs=row_spec,
    )
    return pl.pallas_call(
        functools.partial(_experts_kernel, bm=bm),
        grid_spec=grid_spec,
        out_shape=jax.ShapeDtypeStruct((n_rows * SUBLANES, LANES), F32),
        compiler_params=pltpu.CompilerParams(dimension_semantics=("arbitrary",), vmem_limit_bytes=VMEM_LIMIT),
        name="moe_experts",
    )(blk_e, blk_rows, n_used, xs, w_in, w_out)


def _combine_kernel(dest_ref, x_ref, gate_ref, ys_hbm, ws_in_ref, ws_out_ref, lng_ref, lnb_ref,
                    y_ref, buf, sem, *, tt):
    def issue(t, carry):
        for k in range(TOP_K):
            pltpu.make_async_copy(ys_hbm.at[_tile_of_row(dest_ref[t * TOP_K + k])], buf.at[k, _tile_of_row(t)],
                                  sem).start(priority=k % 2)
        return carry

    lax.fori_loop(0, tt, issue, 0)

    x = x_ref[...]
    h = _dot(x.astype(BF16), ws_in_ref[...])
    act = (_silu(h[:, 0:EXPERT_FF]) * h[:, EXPERT_FF:2 * EXPERT_FF]).astype(BF16)
    shared = _dot(act, ws_out_ref[...])

    for k in range(TOP_K):
        pltpu.make_async_copy(ys_hbm.at[pl.ds(0, tt * SUBLANES)], buf.at[k], sem).wait()
    gates = gate_ref[...]
    routed = _tiles_to_rows(buf.at[0]) * gates[:, 0:1]
    for k in range(1, TOP_K):
        routed = routed + _tiles_to_rows(buf.at[k]) * gates[:, k:k + 1]
    y_ref[...] = _layer_norm(ALPHA * x + (routed + shared), lng_ref[...], lnb_ref[...])


def _combine(x2, dest_flat, gates, ys, ws_in, ws_out, ln_g, ln_b, *, tt):
    t_total = x2.shape[0]
    return pl.pallas_call(
        functools.partial(_combine_kernel, tt=tt),
        grid=(t_total // tt,),
        in_specs=[pl.BlockSpec((tt * TOP_K,), lambda i: (i,), memory_space=pltpu.SMEM),
                  pl.BlockSpec((tt, D_MODEL), lambda i: (i, 0)),
                  pl.BlockSpec((tt, TOP_K), lambda i: (i, 0)),
                  pl.BlockSpec(memory_space=pl.ANY),
                  _const_spec((D_MODEL, 2 * EXPERT_FF)), _const_spec((EXPERT_FF, D_MODEL)),
                  _const_spec((1, D_MODEL)), _const_spec((1, D_MODEL))],
        out_specs=pl.BlockSpec((tt, D_MODEL), lambda i: (i, 0)),
        out_shape=jax.ShapeDtypeStruct((t_total, D_MODEL), F32),
        scratch_shapes=[pltpu.VMEM((TOP_K, tt * SUBLANES, LANES), F32), pltpu.SemaphoreType.DMA(())],
        compiler_params=pltpu.CompilerParams(dimension_semantics=("arbitrary",), vmem_limit_bytes=VMEM_LIMIT),
        name="moe_combine",
    )(dest_flat, x2, gates, ys, ws_in.astype(BF16), ws_out.astype(BF16),
      ln_g.astype(F32)[None, :], ln_b.astype(F32)[None, :])


def _moe_ffn(x, w_r, b_r, w_in, w_out, ws_in, ws_out, ln_g, ln_b, *, tt, bm):
    bsz, l_total, _ = x.shape
    t_total = bsz * l_total
    x2 = x.reshape(t_total, D_MODEL)
    idx_t, gate_t, pos_t, cnt = _route(x2, w_r, b_r, tt=min(ROUTE_TILE, t_total))

    counts = cnt[:, 0]
    padded = (counts + bm - 1) // bm * bm
    pad_end = jnp.cumsum(padded)
    pad_start = pad_end - padded
    onehot = idx_t[:, :, None] == jnp.arange(N_EXPERTS, dtype=I32)
    dest_t = pos_t + jnp.sum(jnp.where(onehot, pad_start, 0), axis=-1)
    dest_flat = dest_t.T.reshape(-1)
    n_blocks = -(-(t_total * TOP_K + N_EXPERTS * (bm - 1)) // bm)
    blk_start = jnp.arange(n_blocks, dtype=I32) * bm
    blk_e = jnp.sum((pad_end[None, :] <= blk_start[:, None]).astype(I32), axis=1)
    blk_e = jnp.minimum(blk_e, N_EXPERTS - 1)
    blk_rows = jnp.clip(pad_start[blk_e] + counts[blk_e] - blk_start, 0, bm).astype(I32)
    n_used = (pad_end[-1:] // bm).astype(I32)

    tt_d = min(tt, t_total)
    xs = _dispatch(x2, dest_flat, n_blocks * bm, tt=tt_d)
    ys = _experts(xs, blk_e, blk_rows, n_used, w_in.astype(BF16), w_out.astype(BF16), bm=bm)
    y = _combine(x2, dest_flat, gate_t.T, ys, ws_in, ws_out, ln_g, ln_b, tt=tt_d)
    return y.reshape(bsz, l_total, D_MODEL)


def _pad_time(t, total):
    return jnp.pad(t, [(0, 0), (0, total - t.shape[1]), (0, 0)])


def _trunk(x, pos0, conv_st, ssm_st, pool_st, hgrn_st, weights, *, tl, tt, bm):
    (w_in_l0, conv_w_l0, conv_b_l0, dt_bias_l0, a_log_l0, d_skip_l0, ssd_norm_l0, w_pool_l0,
     pool_scale_l0, w_out_l0, w_in_l1, lb_raw, hgrn_norm_l1, w_out_l1, ln_mix_g, ln_mix_b,
     ln_ffn_g, ln_ffn_b, router_w, router_bias, moe_w_in, moe_w_out, shared_w_in, shared_w_out) = weights
    bsz, l_valid, _ = x.shape
    l_total = -(-l_valid // tl) * tl
    xp = _pad_time(x, l_total)

    lb_sm = jax.nn.softmax(lb_raw.astype(F32), axis=0)
    lb_all = jnp.cumsum(lb_sm, axis=0) - lb_sm[0]

    conv0 = jnp.pad(conv_st.astype(F32), [(0, 0), (8 - (SSD_CONV_W - 1), 0), (0, 0)])
    pool0 = jnp.pad(pool_st.astype(F32), [(0, 0), (1, 0), (0, 0)])
    ssm0_t = ssm_st.astype(F32).reshape(bsz, SSD_D_INNER, SSD_STATE).transpose(0, 2, 1)
    p0 = _l0_params(w_in_l0, conv_w_l0, conv_b_l0, dt_bias_l0, a_log_l0, d_skip_l0, ssd_norm_l0,
                    w_pool_l0, pool_scale_l0, w_out_l0, ln_mix_g[0], ln_mix_b[0])
    h, conv_o, ssm_o, pool_o = _l0_mixer(xp, conv0, ssm0_t, pool0, p0, tl=tl, l_valid=l_valid, pos0=pos0)
    new_conv = conv_o[:, 8 - (SSD_CONV_W - 1):]
    new_pool = pool_o[:, 1:]
    new_ssm = ssm_o.transpose(0, 2, 1).reshape(bsz, SSD_HEADS, SSD_HEAD_DIM, SSD_STATE)
    h = _moe_ffn(h[:, :l_valid], router_w[0], router_bias[0], moe_w_in[0], moe_w_out[0], shared_w_in[0],
                 shared_w_out[0], ln_ffn_g[0], ln_ffn_b[0], tt=tt, bm=bm)

    p1 = _l1_params(w_in_l1, lb_all[1], hgrn_norm_l1, w_out_l1, ln_mix_g[1], ln_mix_b[1])
    h, hgrn_o = _l1_mixer(_pad_time(h, l_total), hgrn_st.astype(F32).transpose(0, 1, 3, 2), p1,
                          tl=tl, l_valid=l_valid)
    new_hgrn = hgrn_o.transpose(0, 1, 3, 2)
    h = _moe_ffn(h[:, :l_valid], router_w[1], router_bias[1], moe_w_in[1], moe_w_out[1], shared_w_in[1],
                 shared_w_out[1], ln_ffn_g[1], ln_ffn_b[1], tt=tt, bm=bm)
    return h, new_conv, new_ssm, new_pool, new_hgrn


PROMPT_TL = 256
SAMPLE_TL = 128
TOKEN_TILE = 256
EXPERT_BLOCK = 512


def kernel(x_prompt, x_sample, state_conv_l0, state_ssm_l0, state_pool_l0, state_hgrn_l1, w_in_l0, conv_w_l0, conv_b_l0, dt_bias_l0, a_log_l0, d_skip_l0, ssd_norm_l0, w_pool_l0, pool_scale_l0, w_out_l0, w_in_l1, lb_raw, hgrn_norm_l1, w_out_l1, ln_mix_g, ln_mix_b, ln_ffn_g, ln_ffn_b, router_w, router_bias, moe_w_in, moe_w_out, shared_w_in, shared_w_out):
    weights = (w_in_l0, conv_w_l0, conv_b_l0, dt_bias_l0, a_log_l0, d_skip_l0, ssd_norm_l0, w_pool_l0,
               pool_scale_l0, w_out_l0, w_in_l1, lb_raw, hgrn_norm_l1, w_out_l1, ln_mix_g, ln_mix_b,
               ln_ffn_g, ln_ffn_b, router_w, router_bias, moe_w_in, moe_w_out, shared_w_in, shared_w_out)
    bp = x_prompt.shape[0]
    past_len = 1024
    zeros = lambda *s: jnp.zeros(s, F32)
    y_p, p_conv, p_ssm, p_pool, p_hgrn = _trunk(
        x_prompt, 0, zeros(bp, SSD_CONV_W - 1, SSD_CONV_DIM), zeros(bp, SSD_HEADS, SSD_HEAD_DIM, SSD_STATE),
        zeros(bp, POOL_MAXW - 1, POOL_DIM), zeros(bp, HGRN_HEADS, HGRN_DK, HGRN_DV), weights,
        tl=PROMPT_TL, tt=TOKEN_TILE, bm=EXPERT_BLOCK)
    y_s, s_conv, s_ssm, s_pool, s_hgrn = _trunk(
        x_sample, past_len, state_conv_l0, state_ssm_l0, state_pool_l0, state_hgrn_l1, weights,
        tl=SAMPLE_TL, tt=TOKEN_TILE, bm=EXPERT_BLOCK)
    return (y_p, y_s, p_conv, p_ssm, p_pool, p_hgrn, s_conv, s_ssm, s_pool, s_hgrn)
```

```python
import functools
import math

import jax
import jax.numpy as jnp
from jax import lax
from jax.experimental import pallas as pl
from jax.experimental.pallas import tpu as pltpu

F32 = jnp.float32
BF16 = jnp.bfloat16
I32 = jnp.int32
U32 = jnp.uint32
HIGHEST = lax.Precision.HIGHEST

D_MODEL = 1024
DEPTH = 2
ALPHA = (2 * DEPTH) ** 0.25
EPS = 1e-5

SSD_HEADS = 16
SSD_HEAD_DIM = 64
SSD_GROUPS = 2
SSD_STATE = 128
SSD_D_INNER = 1024
SSD_CONV_DIM = 1536
SSD_CONV_W = 4
SSD_Q = 64
GROUP_CH = SSD_D_INNER // SSD_GROUPS

POOL_DIM = 1024
POOL_WINDOWS = (2, 4, 8, 16)
POOL_GROUP_DIM = 256
POOL_MAXW = 16

HGRN_HEADS = 8
HGRN_DK = 128
HGRN_DV = 128
HGRN_C = 128
HGRN_SUB = 16

N_EXPERTS = 64
N_EXPERT_GROUPS = 8
GROUP_SIZE = N_EXPERTS // N_EXPERT_GROUPS
TOPK_GROUPS = 4
TOP_K = 8
EXPERT_FF = 256
ROUTED_SCALE = 2.5
ROUTE_TILE = 512

V7X_VMEM_BYTES = 64 * 1024 * 1024
VMEM_LIMIT = V7X_VMEM_BYTES - 8 * 1024 * 1024
LANES = 128
SUBLANES = 8
HALF_D = D_MODEL // 2
PACK_ROWS = HALF_D // LANES

NT_DIMS = (((1,), (1,)), ((), ()))
TN_DIMS = (((0,), (0,)), ((), ()))


def _dot(a, b, precision=None):
    return jnp.dot(a, b, preferred_element_type=F32, precision=precision)


def _dot_nt(a, b, precision=None):
    return lax.dot_general(a, b, NT_DIMS, preferred_element_type=F32, precision=precision)


def _dot_tn(a, b):
    return lax.dot_general(a, b, TN_DIMS, preferred_element_type=F32)


def _softplus(v):
    return jnp.maximum(v, 0.0) + jnp.log1p(jnp.exp(-jnp.abs(v)))


def _silu(v):
    return v * jax.nn.sigmoid(v)


def _layer_norm(h, g, b):
    mu = jnp.mean(h, axis=-1, keepdims=True)
    hc = h - mu
    var = jnp.mean(hc * hc, axis=-1, keepdims=True)
    return hc * lax.rsqrt(var + EPS) * g + b


def _iota(shape, dim):
    return lax.broadcasted_iota(I32, shape, dim)


def _l0_kernel(x_ref, conv0_ref, ssm0_ref, pool0_ref,
               w_in_ref, w_dtt_ref, conv_w_ref, conv_b_ref, dtb_x_ref, dtb_c_ref, a_x_ref, a_c_ref,
               dskip_ref, normw_ref, w_pool_ref, pscale_ref, w_out_ref, lng_ref, lnb_ref,
               y_ref, conv_o_ref, ssm_o_ref, pool_o_ref,
               xbc_buf, u_buf, xa_buf, z_buf, dtx_buf, h_ref, mix_buf,
               *, tl, l_total, l_valid, pos0):
    j = pl.program_id(1)
    nt = l_total // tl
    q = SSD_Q
    nq = tl // q
    masked = l_valid < l_total

    @pl.when(j == 0)
    def _():
        xbc_buf[0:8, :] = conv0_ref[0]
        u_buf[0:16, :] = pool0_ref[0]
        h_ref[...] = ssm0_ref[0]

    x = x_ref[0]
    xb = x.astype(BF16)

    z_buf[...] = _dot(xb, w_in_ref[:, 0:1024])
    xbc_buf[8:8 + tl, :] = _dot(xb, w_in_ref[:, 1024:2560])
    u_buf[16:16 + tl, :] = _dot(xb, w_in_ref[:, 2560:3584])
    dtx = _softplus(_dot(xb, w_in_ref[:, 3584:4608]) + dtb_x_ref[...])
    if masked:
        t_col = j * tl + _iota((tl, 1), 0)
        dtx = jnp.where(t_col < l_valid, dtx, 0.0)
    dtx_buf[...] = dtx

    acc = conv_b_ref[...] + conv_w_ref[3:4, :] * xbc_buf[8:8 + tl, :]
    for jj in range(SSD_CONV_W - 1):
        acc = acc + conv_w_ref[jj:jj + 1, :] * xbc_buf[5 + jj:5 + jj + tl, :]
    xa_buf[...] = _silu(acc)

    a_x = a_x_ref[...]
    a_c = a_c_ref[...]
    dskip = dskip_ref[...]
    normw = normw_ref[...]

    ltri = (_iota((q, q), 1) <= _iota((q, q), 0)).astype(F32)
    sp = _iota((q, LANES), 0)
    ln = _iota((q, LANES), 1)
    mt_left = ((ln < q) & (sp <= ln)).astype(F32)
    mt_right = ((ln >= q) & (sp <= ln - q)).astype(F32)
    causal2 = (ln % q) <= sp
    low_half = ln < q

    def chunk(c, carry):
        r0 = pl.multiple_of(c * q, q)
        rows = pl.ds(r0, q)
        xs = xa_buf[rows, 0:1024]
        dtc = dtx_buf[rows, :]
        cum = _dot(ltri, dtc * a_x, HIGHEST)
        ecum = jnp.exp(cum)
        cum_last = cum[q - 1:q, :]
        xw = xs * (jnp.exp(cum_last - cum) * dtc)
        xdt = xs * dtc

        xcb = x_ref[0, rows, :].astype(BF16)
        dtt = _softplus(_dot_nt(w_dtt_ref[...], xcb) + dtb_c_ref[...])
        if masked:
            t_row = j * tl + r0 + _iota((1, q), 1)
            dtt = jnp.where(t_row < l_valid, dtt, 0.0)
        dat = dtt * a_c
        cum_t2 = _dot(dat[0:8, :], mt_left, HIGHEST) + _dot(dat[8:16, :], mt_right, HIGHEST)

        pieces = []
        for g in range(SSD_GROUPS):
            gs = slice(g * GROUP_CH, (g + 1) * GROUP_CH)
            bg = xa_buf[rows, 1024 + g * SSD_STATE:1024 + (g + 1) * SSD_STATE].astype(BF16)
            cg = xa_buf[rows, 1280 + g * SSD_STATE:1280 + (g + 1) * SSD_STATE].astype(BF16)
            cb2 = _dot_nt(cg, jnp.concatenate([bg, bg], axis=0))
            h_g = h_ref[:, gs]
            y_off = _dot(cg, h_g.astype(BF16)) * ecum[:, gs]
            for pi in range(GROUP_CH // LANES):
                i = g * (GROUP_CH // LANES) + pi
                ls = slice(i * LANES, (i + 1) * LANES)
                seg = cum[:, ls] - cum_t2[i:i + 1, :]
                mixp = (cb2 * jnp.exp(jnp.where(causal2, seg, -jnp.inf))).astype(BF16)
                xp = xdt[:, ls]
                wblk = jnp.concatenate([jnp.where(low_half, xp, 0.0), jnp.where(low_half, 0.0, xp)],
                                       axis=0).astype(BF16)
                y_pair = (_dot(mixp, wblk) + y_off[:, pi * LANES:(pi + 1) * LANES]
                          + xs[:, ls] * dskip[:, ls])
                pieces.append(y_pair)
            h_ref[:, gs] = h_g * ecum[q - 1:q, gs] + _dot_tn(bg, xw[:, gs].astype(BF16))

        for g in range(SSD_GROUPS):
            gs = slice(g * GROUP_CH, (g + 1) * GROUP_CH)
            yg = jnp.concatenate(pieces[g * 4:(g + 1) * 4], axis=1) * _silu(z_buf[rows, gs])
            ms = jnp.mean(yg * yg, axis=-1, keepdims=True)
            mix_buf[rows, gs] = (yg * lax.rsqrt(ms + EPS) * normw[:, gs]).astype(BF16)
        return carry

    lax.fori_loop(0, nq, chunk, 0, unroll=True)

    pos = (pos0 + j * tl + _iota((tl, 1), 0)).astype(F32)
    for gi, w in enumerate(POOL_WINDOWS):
        cs = slice(gi * POOL_GROUP_DIM, (gi + 1) * POOL_GROUP_DIM)
        cur = u_buf[:, cs]
        d = 1
        while d < w:
            cur = cur + pltpu.roll(cur, d, 0)
            d *= 2
        cnt = jnp.minimum(pos + 1.0, float(w))
        diff = cur[16:, :] / cnt - u_buf[16:16 + tl, cs]
        yb = _dot(diff.astype(BF16), w_pool_ref[gi]) * pscale_ref[:, cs]
        mix_buf[:, SSD_D_INNER + gi * POOL_GROUP_DIM:SSD_D_INNER + (gi + 1) * POOL_GROUP_DIM] = yb.astype(BF16)

    mix = _dot(mix_buf[...], w_out_ref[...])
    y_ref[0] = _layer_norm(ALPHA * x + mix, lng_ref[...], lnb_ref[...])

    lv = l_valid - (nt - 1) * tl

    @pl.when(j == nt - 1)
    def _():
        conv_o_ref[0] = xbc_buf[lv:lv + 8, :]
        pool_o_ref[0] = u_buf[lv:lv + 16, :]
        ssm_o_ref[0] = h_ref[...]

    @pl.when(j < nt - 1)
    def _():
        xbc_buf[0:8, :] = xbc_buf[tl:tl + 8, :]
        u_buf[0:16, :] = u_buf[tl:tl + 16, :]


def _const_spec(shape):
    nd = len(shape)
    return pl.BlockSpec(shape, lambda *_: (0,) * nd)


def _l0_mixer(x, conv0, ssm0_t, pool0, params, *, tl, l_valid, pos0):
    bsz, l_total, _ = x.shape
    nt = l_total // tl
    per_b3 = lambda b, j: (b, 0, 0)
    in_specs = [
        pl.BlockSpec((1, tl, D_MODEL), lambda b, j: (b, j, 0)),
        pl.BlockSpec((1, 8, SSD_CONV_DIM), per_b3),
        pl.BlockSpec((1, SSD_STATE, SSD_D_INNER), per_b3),
        pl.BlockSpec((1, 16, POOL_DIM), per_b3),
    ] + [_const_spec(p.shape) for p in params]
    out_shape = (
        jax.ShapeDtypeStruct((bsz, l_total, D_MODEL), F32),
        jax.ShapeDtypeStruct((bsz, 8, SSD_CONV_DIM), F32),
        jax.ShapeDtypeStruct((bsz, SSD_STATE, SSD_D_INNER), F32),
        jax.ShapeDtypeStruct((bsz, 16, POOL_DIM), F32),
    )
    out_specs = (
        pl.BlockSpec((1, tl, D_MODEL), lambda b, j: (b, j, 0)),
        pl.BlockSpec((1, 8, SSD_CONV_DIM), per_b3),
        pl.BlockSpec((1, SSD_STATE, SSD_D_INNER), per_b3),
        pl.BlockSpec((1, 16, POOL_DIM), per_b3),
    )
    scratch = [
        pltpu.VMEM((8 + tl, SSD_CONV_DIM), F32),
        pltpu.VMEM((16 + tl, POOL_DIM), F32),
        pltpu.VMEM((tl, SSD_CONV_DIM), F32),
        pltpu.VMEM((tl, SSD_D_INNER), F32),
        pltpu.VMEM((tl, SSD_D_INNER), F32),
        pltpu.VMEM((SSD_STATE, SSD_D_INNER), F32),
        pltpu.VMEM((tl, 2 * D_MODEL), BF16),
    ]
    return pl.pallas_call(
        functools.partial(_l0_kernel, tl=tl, l_total=l_total, l_valid=l_valid, pos0=pos0),
        grid=(bsz, nt),
        in_specs=in_specs,
        out_specs=out_specs,
        out_shape=out_shape,
        scratch_shapes=scratch,
        compiler_params=pltpu.CompilerParams(
            dimension_semantics=("arbitrary", "arbitrary"), vmem_limit_bytes=VMEM_LIMIT),
        name="l0_mixer",
    )(x, conv0, ssm0_t, pool0, *params)


def _l0_params(w_in, conv_w, conv_b, dt_bias, a_log, d_skip, ssd_norm, w_pool, pool_scale, w_out, ln_g, ln_b):
    rep = lambda v: jnp.repeat(v.astype(F32), SSD_HEAD_DIM)[None, :]
    z0, z1 = 0, SSD_D_INNER
    x1 = z1 + SSD_CONV_DIM
    d1 = x1 + SSD_HEADS
    w_z, w_xbc, w_dt, w_u = w_in[:, z0:z1], w_in[:, z1:x1], w_in[:, x1:d1], w_in[:, d1:]
    w_dtx = jnp.repeat(w_dt, SSD_HEAD_DIM, axis=1)
    w_all = jnp.concatenate([w_z, w_xbc, w_u, w_dtx], axis=1).astype(BF16)
    perm = jnp.concatenate([jnp.arange(0, SSD_HEADS, 2), jnp.arange(1, SSD_HEADS, 2)])
    a = -jnp.exp(a_log.astype(F32))
    return (
        w_all,
        w_dt.T[perm].astype(BF16),
        conv_w.astype(F32), conv_b.astype(F32)[None, :],
        rep(dt_bias), dt_bias.astype(F32)[perm][:, None],
        rep(a), a[perm][:, None],
        rep(d_skip), ssd_norm.astype(F32)[None, :],
        w_pool.astype(BF16), pool_scale.astype(F32)[None, :],
        w_out.astype(BF16), ln_g.astype(F32)[None, :], ln_b.astype(F32)[None, :],
    )


def _l1_kernel(x_ref, st0_ref, w_in_ref, lb_ref, normw_ref, w_out_ref, lng_ref, lnb_ref,
               y_ref, st_o_ref,
               q_buf, k_buf, lf_buf, v_buf, gt_buf, g_buf, s_ref, mix_buf,
               *, tl, l_total, l_valid):
    j = pl.program_id(1)
    nt = l_total // tl
    c = HGRN_C
    nc = tl // c
    nsub = c // HGRN_SUB
    masked = l_valid < l_total

    @pl.when(j == 0)
    def _():
        s_ref[...] = st0_ref[0]

    x = x_ref[0]
    xb = x.astype(BF16)
    lb = lb_ref[...]
    q_raw = _dot(xb, w_in_ref[:, 0:1024])
    fr = _dot(xb, w_in_ref[:, 1024:2048])
    v_buf[...] = _dot(xb, w_in_ref[:, 2048:3072])
    gt_buf[...] = _silu(_dot(xb, w_in_ref[:, 3072:4096]))

    log_sig = jnp.minimum(fr, 0.0) - jnp.log1p(jnp.exp(-jnp.abs(fr)))
    t1 = jnp.log(lb)
    t2 = jnp.log1p(-lb) + log_sig
    log_f = jnp.maximum(t1, t2) + jnp.log1p(jnp.exp(-jnp.abs(t1 - t2)))
    kk = (1.0 - lb) * jax.nn.sigmoid(-fr)
    if masked:
        ok = (j * tl + _iota((tl, 1), 0)) < l_valid
        log_f = jnp.where(ok, log_f, 0.0)
        kk = jnp.where(ok, kk, 0.0)
    lf_buf[...] = log_f
    k_buf[...] = kk
    q_buf[...] = _silu(q_raw)

    ltri = (_iota((c, c), 1) <= _iota((c, c), 0)).astype(F32)
    s_col = _iota((HGRN_SUB, c), 1)
    l_row = _iota((HGRN_SUB, c), 0)
    normw = normw_ref[...]

    def chunk(ci, carry):
        r0 = pl.multiple_of(ci * c, c)
        rows = pl.ds(r0, c)
        g_all = _dot(ltri, lf_buf[rows, :], HIGHEST)
        g_buf[...] = g_all
        g_last = g_all[c - 1:c, :]
        q_c = q_buf[rows, :]
        k_c = k_buf[rows, :]
        qt = q_c * jnp.exp(g_all)
        kd = k_c * jnp.exp(g_last - g_all)
        for h in range(HGRN_HEADS):
            hs = slice(h * HGRN_DK, (h + 1) * HGRN_DK)
            st = s_ref[h]
            v_h = v_buf[rows, hs].astype(BF16)
            o_h = _dot_nt(qt[:, hs].astype(BF16), st.astype(BF16))
            sc_rows = []
            for si in range(nsub):
                lo, hi = si * HGRN_SUB, (si + 1) * HGRN_SUB
                g_sub = g_buf[lo:hi, hs]
                if si == 0:
                    a_i = q_c[lo:hi, hs] * jnp.exp(g_sub)
                    b_i = k_c[0:hi, hs] * jnp.exp(-g_buf[0:hi, hs])
                else:
                    g_ref_row = g_buf[lo - 1:lo, hs]
                    a_i = q_c[lo:hi, hs] * jnp.exp(g_sub - g_ref_row)
                    b_i = k_c[0:hi, hs] * jnp.exp(g_ref_row - g_buf[0:hi, hs])
                b_i = b_i.astype(BF16)
                if hi < c:
                    b_i = jnp.concatenate([b_i, jnp.zeros((c - hi, HGRN_DK), BF16)], axis=0)
                sc = _dot_nt(a_i.astype(BF16), b_i)
                sc_rows.append(jnp.where(s_col <= l_row + lo, sc, 0.0))
            scores = jnp.concatenate(sc_rows, axis=0).astype(BF16)
            o_h = o_h + _dot(scores, v_h)
            s_ref[h] = st * jnp.exp(g_last[:, hs]) + _dot_tn(v_h, kd[:, hs].astype(BF16))
            ms = jnp.mean(o_h * o_h, axis=-1, keepdims=True)
            mix_buf[rows, hs] = (o_h * lax.rsqrt(ms + EPS) * normw * gt_buf[rows, hs]).astype(BF16)
        return carry

    lax.fori_loop(0, nc, chunk, 0, unroll=True)

    mix = _dot(mix_buf[...], w_out_ref[...])
    y_ref[0] = _layer_norm(ALPHA * x + mix, lng_ref[...], lnb_ref[...])

    @pl.when(j == nt - 1)
    def _():
        st_o_ref[0] = s_ref[...]


def _l1_mixer(x, st0_t, params, *, tl, l_valid):
    bsz, l_total, _ = x.shape
    nt = l_total // tl
    st_spec = pl.BlockSpec((1, HGRN_HEADS, HGRN_DV, HGRN_DK), lambda b, j: (b, 0, 0, 0))
    in_specs = [pl.BlockSpec((1, tl, D_MODEL), lambda b, j: (b, j, 0)), st_spec]
    in_specs += [_const_spec(p.shape) for p in params]
    big = lambda: pltpu.VMEM((tl, D_MODEL), F32)
    return pl.pallas_call(
        functools.partial(_l1_kernel, tl=tl, l_total=l_total, l_valid=l_valid),
        grid=(bsz, nt),
        in_specs=in_specs,
        out_specs=(pl.BlockSpec((1, tl, D_MODEL), lambda b, j: (b, j, 0)), st_spec),
        out_shape=(jax.ShapeDtypeStruct((bsz, l_total, D_MODEL), F32),
                   jax.ShapeDtypeStruct((bsz, HGRN_HEADS, HGRN_DV, HGRN_DK), F32)),
        scratch_shapes=[big(), big(), big(), big(), big(),
                        pltpu.VMEM((HGRN_C, D_MODEL), F32),
                        pltpu.VMEM((HGRN_HEADS, HGRN_DV, HGRN_DK), F32),
                        pltpu.VMEM((tl, D_MODEL), BF16)],
        compiler_params=pltpu.CompilerParams(
            dimension_semantics=("arbitrary", "arbitrary"), vmem_limit_bytes=VMEM_LIMIT),
        name="l1_mixer",
    )(x, st0_t, *params)


def _l1_params(w_in, lb, hgrn_norm, w_out, ln_g, ln_b):
    return (w_in.astype(BF16), lb.astype(F32)[None, :], hgrn_norm.astype(F32)[None, :],
            w_out.astype(BF16), ln_g.astype(F32)[None, :], ln_b.astype(F32)[None, :])


def _route_kernel(x_ref, wrt_ref, bias_ref, idx_ref, gate_ref, pos_ref, cnt_ref, count_sc, *, tt):
    i = pl.program_id(0)

    @pl.when(i == 0)
    def _():
        count_sc[...] = jnp.zeros_like(count_sc)

    neg = -jnp.inf
    scores = jax.nn.sigmoid(_dot_nt(wrt_ref[...], x_ref[...], HIGHEST))
    sel3 = (scores + bias_ref[...]).reshape(N_EXPERT_GROUPS, GROUP_SIZE, tt)

    im = _iota(sel3.shape, 1)
    m1 = jnp.max(sel3, axis=1, keepdims=True)
    i1 = jnp.min(jnp.where(sel3 == m1, im, GROUP_SIZE), axis=1, keepdims=True)
    m2 = jnp.max(jnp.where(im == i1, neg, sel3), axis=1, keepdims=True)
    grp = (m1 + m2).reshape(N_EXPERT_GROUPS, tt)

    ig = _iota(grp.shape, 0)
    keep = jnp.zeros(grp.shape, F32)
    for _ in range(TOPK_GROUPS):
        gm = jnp.max(grp, axis=0, keepdims=True)
        hit = ig == jnp.min(jnp.where(grp == gm, ig, N_EXPERT_GROUPS), axis=0, keepdims=True)
        keep = jnp.where(hit, 1.0, keep)
        grp = jnp.where(hit, neg, grp)

    selm = jnp.where(keep.reshape(N_EXPERT_GROUPS, 1, tt) > 0.0, sel3, neg).reshape(N_EXPERTS, tt)
    ie = _iota(selm.shape, 0)
    membf = jnp.zeros(selm.shape, F32)
    idx_rows, w_rows = [], []
    for _ in range(TOP_K):
        m = jnp.max(selm, axis=0, keepdims=True)
        ei = jnp.min(jnp.where(selm == m, ie, N_EXPERTS), axis=0, keepdims=True)
        hit = ie == ei
        idx_rows.append(ei)
        w_rows.append(jnp.sum(jnp.where(hit, scores, 0.0), axis=0, keepdims=True))
        selm = jnp.where(hit, neg, selm)
        membf = jnp.where(hit, 1.0, membf)
    idx = jnp.concatenate(idx_rows, axis=0)
    w = jnp.concatenate(w_rows, axis=0)
    gate_ref[...] = w / jnp.sum(w, axis=0, keepdims=True) * ROUTED_SCALE
    idx_ref[...] = idx

    before =(_iota((tt, tt), 0) < _iota((tt, tt), 1)).astype(BF16)
    pos_all = _dot(membf.astype(BF16), before) + count_sc[:, 0:1]
    pos_rows = [jnp.sum(jnp.where(ie == idx_rows[k], pos_all, 0.0), axis=0, keepdims=True)
                for k in range(TOP_K)]
    pos_ref[...] = jnp.concatenate(pos_rows, axis=0).astype(I32)
    count_sc[...] = count_sc[...] + jnp.sum(membf, axis=1, keepdims=True)
    cnt_ref[...] = count_sc[...].astype(I32)


def _route(x2, w_r, b_r, *, tt):
    t_total = x2.shape[0]
    row_spec = pl.BlockSpec((TOP_K, tt), lambda i: (0, i))
    return pl.pallas_call(
        functools.partial(_route_kernel, tt=tt),
        grid=(t_total // tt,),
        in_specs=[pl.BlockSpec((tt, D_MODEL), lambda i: (i, 0)),
                  _const_spec((N_EXPERTS, D_MODEL)), _const_spec((N_EXPERTS, 1))],
        out_specs=(row_spec, row_spec, row_spec, _const_spec((N_EXPERTS, LANES))),
        out_shape=(jax.ShapeDtypeStruct((TOP_K, t_total), I32),
                   jax.ShapeDtypeStruct((TOP_K, t_total), F32),
                   jax.ShapeDtypeStruct((TOP_K, t_total), I32),
                   jax.ShapeDtypeStruct((N_EXPERTS, LANES), I32)),
        scratch_shapes=[pltpu.VMEM((N_EXPERTS, LANES), F32)],
        compiler_params=pltpu.CompilerParams(dimension_semantics=("arbitrary",), vmem_limit_bytes=VMEM_LIMIT),
        name="moe_route",
    )(x2, w_r.astype(F32).T, b_r.astype(F32)[:, None])


HIGH_HALF = 0xFFFF0000


def _pack_rows(dst_ref, val):
    n = val.shape[0]
    bits = lax.bitcast_convert_type(val.astype(BF16).astype(F32), U32)
    words = (bits[:, HALF_D:] & jnp.uint32(HIGH_HALF)) | (bits[:, :HALF_D] >> 16)
    for jj in range(PACK_ROWS):
        dst_ref[pl.ds(jj, n, stride=PACK_ROWS), :] = words[:, jj * LANES:(jj + 1) * LANES]


def _unpack_rows(src_ref):
    n = src_ref.shape[0] // PACK_ROWS
    words = [src_ref[pl.ds(jj, n, stride=PACK_ROWS), :] for jj in range(PACK_ROWS)]
    low = [lax.bitcast_convert_type(w << 16, F32) for w in words]
    high = [lax.bitcast_convert_type(w & jnp.uint32(HIGH_HALF), F32) for w in words]
    return jnp.concatenate(low + high, axis=1)


def _packed_row(r):
    return pl.ds(pl.multiple_of(r * PACK_ROWS, PACK_ROWS), PACK_ROWS)


def _dispatch_kernel(dest_ref, x_ref, xs_hbm, xp, sem, *, tt):
    _pack_rows(xp, x_ref[...])

    def issue(t, carry):
        for k in range(TOP_K):
            pltpu.make_async_copy(xp.at[_packed_row(t)], xs_hbm.at[_packed_row(dest_ref[t * TOP_K + k])],
                                  sem).start(priority=k % 2)
        return carry

    lax.fori_loop(0, tt, issue, 0)
    for _ in range(TOP_K):
        pltpu.make_async_copy(xp, xs_hbm.at[pl.ds(0, tt * PACK_ROWS)], sem).wait()


def _dispatch(x2, dest_flat, n_rows, *, tt):
    t_total = x2.shape[0]
    return pl.pallas_call(
        functools.partial(_dispatch_kernel, tt=tt),
        grid=(t_total // tt,),
        in_specs=[pl.BlockSpec((tt * TOP_K,), lambda i: (i,), memory_space=pltpu.SMEM),
                  pl.BlockSpec((tt, D_MODEL), lambda i: (i, 0))],
        out_specs=pl.BlockSpec(memory_space=pl.ANY),
        out_shape=jax.ShapeDtypeStruct((n_rows * PACK_ROWS, LANES), U32),
        scratch_shapes=[pltpu.VMEM((tt * PACK_ROWS, LANES), U32), pltpu.SemaphoreType.DMA(())],
        compiler_params=pltpu.CompilerParams(dimension_semantics=("arbitrary",), vmem_limit_bytes=VMEM_LIMIT),
        name="moe_dispatch",
    )(dest_flat, x2)


def _experts_kernel(blk_e_ref, blk_rows_ref, n_used_ref, xs_ref, w_in_ref, w_out_ref, ys_ref,
                    w_in_bf, w_out_bf, *, bm):
    i = pl.program_id(0)

    @pl.when(i < n_used_ref[0])
    def _():
        @pl.when((i == 0) | (blk_e_ref[i] != blk_e_ref[jnp.maximum(i - 1, 0)]))
        def _():
            w_in_bf[...] = w_in_ref[0].astype(BF16)
            w_out_bf[...] = w_out_ref[0].astype(BF16)

        live = _iota((bm, 1), 0) < blk_rows_ref[i]
        xb = jnp.where(live, _unpack_rows(xs_ref), 0.0).astype(BF16)
        h = _dot(xb, w_in_bf[...])
        act = (_silu(h[:, 0:EXPERT_FF]) * h[:, EXPERT_FF:2 * EXPERT_FF]).astype(BF16)
        _pack_rows(ys_ref, _dot(act, w_out_bf[...]))


def _experts(xs, blk_e, blk_rows, n_used, w_in, w_out, *, bm):
    n_rows = xs.shape[0] // PACK_ROWS
    n_blocks = n_rows // bm
    last = lambda i, be, br, nu: jnp.minimum(i, nu[0] - 1)
    row_spec = pl.BlockSpec((bm * PACK_ROWS, LANES), lambda i, be, br, nu: (last(i, be, br, nu), 0))
    grid_spec = pltpu.PrefetchScalarGridSpec(
        num_scalar_prefetch=3,
        grid=(n_blocks,),
        in_specs=[row_spec,
                  pl.BlockSpec((1, D_MODEL, 2 * EXPERT_FF), lambda i, be, br, nu: (be[last(i, be, br, nu)], 0, 0)),
                  pl.BlockSpec((1, EXPERT_FF, D_MODEL), lambda i, be, br, nu: (be[last(i, be, br, nu)], 0, 0))],
        out_specs=row_spec,
        scratch_shapes=[pltpu.VMEM((D_MODEL, 2 * EXPERT_FF), BF16), pltpu.VMEM((EXPERT_FF, D_MODEL), BF16)],
    )
    return pl.pallas_call(
        functools.partial(_experts_kernel, bm=bm),
        grid_spec=grid_spec,
        out_shape=jax.ShapeDtypeStruct((n_rows * PACK_ROWS, LANES), U32),
        compiler_params=pltpu.CompilerParams(dimension_semantics=("arbitrary",), vmem_limit_bytes=VMEM_LIMIT),
        name="moe_experts",
    )(blk_e, blk_rows, n_used, xs, w_in, w_out)


def _combine_kernel(dest_ref, dest_next_ref, x_ref, gate_ref, ys_hbm, ws_in_ref, ws_out_ref, lng_ref, lnb_ref,
                    y_ref, buf, sems, *, tt, n_tiles):
    i = pl.program_id(0)
    slot = i % 2

    def gather(d_ref, s):
        def issue(t, carry):
            for k in range(TOP_K):
                pltpu.make_async_copy(ys_hbm.at[_packed_row(d_ref[t * TOP_K + k])], buf.at[s, k, _packed_row(t)],
                                      sems.at[s]).start(priority=k % 2)
            return carry

        lax.fori_loop(0, tt, issue, 0)

    @pl.when(i == 0)
    def _():
        gather(dest_ref, 0)

    @pl.when(i + 1 < n_tiles)
    def _():
        gather(dest_next_ref, 1 - slot)

    x = x_ref[...]
    h = _dot(x.astype(BF16), ws_in_ref[...])
    act = (_silu(h[:, 0:EXPERT_FF]) * h[:, EXPERT_FF:2 * EXPERT_FF]).astype(BF16)
    shared = _dot(act, ws_out_ref[...])

    for k in range(TOP_K):
        pltpu.make_async_copy(ys_hbm.at[pl.ds(0, tt * PACK_ROWS)], buf.at[slot, k], sems.at[slot]).wait()
    gates = gate_ref[...]
    routed = _unpack_rows(buf.at[slot, 0]) * gates[:, 0:1]
    for k in range(1, TOP_K):
        routed = routed + _unpack_rows(buf.at[slot, k]) * gates[:, k:k + 1]
    y_ref[...] = _layer_norm(ALPHA * x + (routed + shared), lng_ref[...], lnb_ref[...])


def _combine(x2, dest_flat, gates, ys, ws_in, ws_out, ln_g, ln_b, *, tt):
    t_total = x2.shape[0]
    n_tiles = t_total // tt
    return pl.pallas_call(
        functools.partial(_combine_kernel, tt=tt, n_tiles=n_tiles),
        grid=(n_tiles,),
        in_specs=[pl.BlockSpec((tt * TOP_K,), lambda i: (i,), memory_space=pltpu.SMEM),
                  pl.BlockSpec((tt * TOP_K,), lambda i: (jnp.minimum(i + 1, n_tiles - 1),), memory_space=pltpu.SMEM),
                  pl.BlockSpec((tt, D_MODEL), lambda i: (i, 0)),
                  pl.BlockSpec((tt, TOP_K), lambda i: (i, 0)),
                  pl.BlockSpec(memory_space=pl.ANY),
                  _const_spec((D_MODEL, 2 * EXPERT_FF)), _const_spec((EXPERT_FF, D_MODEL)),
                  _const_spec((1, D_MODEL)), _const_spec((1, D_MODEL))],
        out_specs=pl.BlockSpec((tt, D_MODEL), lambda i: (i, 0)),
        out_shape=jax.ShapeDtypeStruct((t_total, D_MODEL), F32),
        scratch_shapes=[pltpu.VMEM((2, TOP_K, tt * PACK_ROWS, LANES), U32), pltpu.SemaphoreType.DMA((2,))],
        compiler_params=pltpu.CompilerParams(dimension_semantics=("arbitrary",), vmem_limit_bytes=VMEM_LIMIT),
        name="moe_combine",
    )(dest_flat, dest_flat, x2, gates, ys, ws_in.astype(BF16), ws_out.astype(BF16),
      ln_g.astype(F32)[None, :], ln_b.astype(F32)[None, :])


def _moe_ffn(x, w_r, b_r, w_in, w_out, ws_in, ws_out, ln_g, ln_b, *, tt, bm):
    bsz, l_total, _ = x.shape
    t_total = bsz * l_total
    x2 = x.reshape(t_total, D_MODEL)
    idx_t, gate_t, pos_t, cnt = _route(x2, w_r, b_r, tt=min(ROUTE_TILE, t_total))

    counts = cnt[:, 0]
    padded = (counts + bm - 1) // bm * bm
    pad_end = jnp.cumsum(padded)
    pad_start = pad_end - padded
    onehot = idx_t[:, :, None] == jnp.arange(N_EXPERTS, dtype=I32)
    dest_t = pos_t + jnp.sum(jnp.where(onehot, pad_start, 0), axis=-1)
    dest_flat = dest_t.T.reshape(-1)
    n_blocks = -(-(t_total * TOP_K + N_EXPERTS * (bm - 1)) // bm)
    blk_start = jnp.arange(n_blocks, dtype=I32) * bm
    blk_e = jnp.sum((pad_end[None, :] <= blk_start[:, None]).astype(I32), axis=1)
    blk_e = jnp.minimum(blk_e, N_EXPERTS - 1)
    blk_rows = jnp.clip(pad_start[blk_e] + counts[blk_e] - blk_start, 0, bm).astype(I32)
    n_used = (pad_end[-1:] // bm).astype(I32)

    tt_d = min(tt, t_total)
    xs = _dispatch(x2, dest_flat, n_blocks * bm, tt=tt_d)
    ys = _experts(xs, blk_e, blk_rows, n_used, w_in.astype(F32), w_out.astype(F32), bm=bm)
    y = _combine(x2, dest_flat, gate_t.T, ys, ws_in, ws_out, ln_g, ln_b, tt=tt_d)
    return y.reshape(bsz, l_total, D_MODEL)


def _pad_time(t, total):
    return jnp.pad(t, [(0, 0), (0, total - t.shape[1]), (0, 0)])


def _trunk(x, pos0, conv_st, ssm_st, pool_st, hgrn_st, weights, *, tl, tt, bm):
    (w_in_l0, conv_w_l0, conv_b_l0, dt_bias_l0, a_log_l0, d_skip_l0, ssd_norm_l0, w_pool_l0,
     pool_scale_l0, w_out_l0, w_in_l1, lb_raw, hgrn_norm_l1, w_out_l1, ln_mix_g, ln_mix_b,
     ln_ffn_g, ln_ffn_b, router_w, router_bias, moe_w_in, moe_w_out, shared_w_in, shared_w_out) = weights
    bsz, l_valid, _ = x.shape
    l_total = -(-l_valid // tl) * tl
    xp = _pad_time(x, l_total)

    lb_sm = jax.nn.softmax(lb_raw.astype(F32), axis=0)
    lb_all = jnp.cumsum(lb_sm, axis=0) - lb_sm[0]

    conv0 = jnp.pad(conv_st.astype(F32), [(0, 0), (8 - (SSD_CONV_W - 1), 0), (0, 0)])
    pool0 = jnp.pad(pool_st.astype(F32), [(0, 0), (1, 0), (0, 0)])
    ssm0_t = ssm_st.astype(F32).reshape(bsz, SSD_D_INNER, SSD_STATE).transpose(0, 2, 1)
    p0 = _l0_params(w_in_l0, conv_w_l0, conv_b_l0, dt_bias_l0, a_log_l0, d_skip_l0, ssd_norm_l0,
                    w_pool_l0, pool_scale_l0, w_out_l0, ln_mix_g[0], ln_mix_b[0])
    h, conv_o, ssm_o, pool_o = _l0_mixer(xp, conv0, ssm0_t, pool0, p0, tl=tl, l_valid=l_valid, pos0=pos0)
    new_conv = conv_o[:, 8 - (SSD_CONV_W - 1):]
    new_pool = pool_o[:, 1:]
    new_ssm = ssm_o.transpose(0, 2, 1).reshape(bsz, SSD_HEADS, SSD_HEAD_DIM, SSD_STATE)
    h = _moe_ffn(h[:, :l_valid], router_w[0], router_bias[0], moe_w_in[0], moe_w_out[0], shared_w_in[0],
                 shared_w_out[0], ln_ffn_g[0], ln_ffn_b[0], tt=tt, bm=bm)

    p1 = _l1_params(w_in_l1, lb_all[1], hgrn_norm_l1, w_out_l1, ln_mix_g[1], ln_mix_b[1])
    h, hgrn_o = _l1_mixer(_pad_time(h, l_total), hgrn_st.astype(F32).transpose(0, 1, 3, 2), p1,
                          tl=tl, l_valid=l_valid)
    new_hgrn = hgrn_o.transpose(0, 1, 3, 2)
    h = _moe_ffn(h[:, :l_valid], router_w[1], router_bias[1], moe_w_in[1], moe_w_out[1], shared_w_in[1],
                 shared_w_out[1], ln_ffn_g[1], ln_ffn_b[1], tt=tt, bm=bm)
    return h, new_conv, new_ssm, new_pool, new_hgrn


PROMPT_TL = 256
SAMPLE_TL = 128
TOKEN_TILE = 256
EXPERT_BLOCK = 512


def kernel(x_prompt, x_sample, state_conv_l0, state_ssm_l0, state_pool_l0, state_hgrn_l1, w_in_l0, conv_w_l0, conv_b_l0, dt_bias_l0, a_log_l0, d_skip_l0, ssd_norm_l0, w_pool_l0, pool_scale_l0, w_out_l0, w_in_l1, lb_raw, hgrn_norm_l1, w_out_l1, ln_mix_g, ln_mix_b, ln_ffn_g, ln_ffn_b, router_w, router_bias, moe_w_in, moe_w_out, shared_w_in, shared_w_out):
    weights = (w_in_l0, conv_w_l0, conv_b_l0, dt_bias_l0, a_log_l0, d_skip_l0, ssd_norm_l0, w_pool_l0,
               pool_scale_l0, w_out_l0, w_in_l1, lb_raw, hgrn_norm_l1, w_out_l1, ln_mix_g, ln_mix_b,
               ln_ffn_g, ln_ffn_b, router_w, router_bias, moe_w_in, moe_w_out, shared_w_in, shared_w_out)
    bp = x_prompt.shape[0]
    past_len = 1024
    zeros = lambda *s: jnp.zeros(s, F32)
    y_p, p_conv, p_ssm, p_pool, p_hgrn = _trunk(
        x_prompt, 0, zeros(bp, SSD_CONV_W - 1, SSD_CONV_DIM), zeros(bp, SSD_HEADS, SSD_HEAD_DIM, SSD_STATE),
        zeros(bp, POOL_MAXW - 1, POOL_DIM), zeros(bp, HGRN_HEADS, HGRN_DK, HGRN_DV), weights,
        tl=PROMPT_TL, tt=TOKEN_TILE, bm=EXPERT_BLOCK)
    y_s, s_conv, s_ssm, s_pool, s_hgrn = _trunk(
        x_sample, past_len, state_conv_l0, state_ssm_l0, state_pool_l0, state_hgrn_l1, weights,
        tl=SAMPLE_TL, tt=TOKEN_TILE, bm=EXPERT_BLOCK)
    return (y_p, y_s, p_conv, p_ssm, p_pool, p_hgrn, s_conv, s_ssm, s_pool, s_hgrn)
```

```python
import functools
import math

import jax
import jax.numpy as jnp
from jax import lax
from jax.experimental import pallas as pl
from jax.experimental.pallas import tpu as pltpu

F32 = jnp.float32
BF16 = jnp.bfloat16
I32 = jnp.int32
U32 = jnp.uint32
HIGHEST = lax.Precision.HIGHEST

D_MODEL = 1024
DEPTH = 2
ALPHA = (2 * DEPTH) ** 0.25
EPS = 1e-5

SSD_HEADS = 16
SSD_HEAD_DIM = 64
SSD_GROUPS = 2
SSD_STATE = 128
SSD_D_INNER = 1024
SSD_CONV_DIM = 1536
SSD_CONV_W = 4
SSD_Q = 64
GROUP_CH = SSD_D_INNER // SSD_GROUPS

POOL_DIM = 1024
POOL_WINDOWS = (2, 4, 8, 16)
POOL_GROUP_DIM = 256
POOL_MAXW = 16

HGRN_HEADS = 8
HGRN_DK = 128
HGRN_DV = 128
HGRN_C = 128
HGRN_SUB = 16

N_EXPERTS = 64
N_EXPERT_GROUPS = 8
GROUP_SIZE = N_EXPERTS // N_EXPERT_GROUPS
TOPK_GROUPS = 4
TOP_K = 8
EXPERT_FF = 256
ROUTED_SCALE = 2.5
ROUTE_TILE = 512
COMBINE_GROUP = 32

V7X_VMEM_BYTES = 64 * 1024 * 1024
VMEM_LIMIT = V7X_VMEM_BYTES - 8 * 1024 * 1024
LANES = 128
SUBLANES = 8
HALF_D = D_MODEL // 2
PACK_ROWS = HALF_D // LANES

NT_DIMS = (((1,), (1,)), ((), ()))
TN_DIMS = (((0,), (0,)), ((), ()))


def _dot(a, b, precision=None):
    return jnp.dot(a, b, preferred_element_type=F32, precision=precision)


def _dot_nt(a, b, precision=None):
    return lax.dot_general(a, b, NT_DIMS, preferred_element_type=F32, precision=precision)


def _dot_tn(a, b):
    return lax.dot_general(a, b, TN_DIMS, preferred_element_type=F32)


def _softplus(v):
    return jnp.maximum(v, 0.0) + jnp.log1p(jnp.exp(-jnp.abs(v)))


def _silu(v):
    return v * jax.nn.sigmoid(v)


def _layer_norm(h, g, b):
    mu = jnp.mean(h, axis=-1, keepdims=True)
    hc = h - mu
    var = jnp.mean(hc * hc, axis=-1, keepdims=True)
    return hc * lax.rsqrt(var + EPS) * g + b


def _iota(shape, dim):
    return lax.broadcasted_iota(I32, shape, dim)


def _l0_kernel(x_ref, conv0_ref, ssm0_ref, pool0_ref,
               w_in_ref, w_dtt_ref, conv_w_ref, conv_b_ref, dtb_x_ref, dtb_c_ref, a_x_ref, a_c_ref,
               dskip_ref, normw_ref, w_pool_ref, pscale_ref, w_out_ref, lng_ref, lnb_ref,
               y_ref, conv_o_ref, ssm_o_ref, pool_o_ref,
               xbc_buf, u_buf, xa_buf, z_buf, dtx_buf, h_ref, mix_buf,
               *, tl, l_total, l_valid, pos0):
    j = pl.program_id(1)
    nt = l_total // tl
    q = SSD_Q
    nq = tl // q
    masked = l_valid < l_total

    @pl.when(j == 0)
    def _():
        xbc_buf[0:8, :] = conv0_ref[0]
        u_buf[0:16, :] = pool0_ref[0]
        h_ref[...] = ssm0_ref[0]

    x = x_ref[0]
    xb = x.astype(BF16)

    z_buf[...] = _dot(xb, w_in_ref[:, 0:1024])
    xbc_buf[8:8 + tl, :] = _dot(xb, w_in_ref[:, 1024:2560])
    u_buf[16:16 + tl, :] = _dot(xb, w_in_ref[:, 2560:3584])
    dtx = _softplus(_dot(xb, w_in_ref[:, 3584:4608]) + dtb_x_ref[...])
    if masked:
        t_col = j * tl + _iota((tl, 1), 0)
        dtx = jnp.where(t_col < l_valid, dtx, 0.0)
    dtx_buf[...] = dtx

    acc = conv_b_ref[...] + conv_w_ref[3:4, :] * xbc_buf[8:8 + tl, :]
    for jj in range(SSD_CONV_W - 1):
        acc = acc + conv_w_ref[jj:jj + 1, :] * xbc_buf[5 + jj:5 + jj + tl, :]
    xa_buf[...] = _silu(acc)

    a_x = a_x_ref[...]
    a_c = a_c_ref[...]
    dskip = dskip_ref[...]
    normw = normw_ref[...]

    ltri = (_iota((q, q), 1) <= _iota((q, q), 0)).astype(F32)
    sp = _iota((q, LANES), 0)
    ln = _iota((q, LANES), 1)
    mt_left = ((ln < q) & (sp <= ln)).astype(F32)
    mt_right = ((ln >= q) & (sp <= ln - q)).astype(F32)
    causal2 = (ln % q) <= sp
    low_half = ln < q

    def chunk(c, carry):
        r0 = pl.multiple_of(c * q, q)
        rows = pl.ds(r0, q)
        xs = xa_buf[rows, 0:1024]
        dtc = dtx_buf[rows, :]
        cum = _dot(ltri, dtc * a_x, HIGHEST)
        ecum = jnp.exp(cum)
        cum_last = cum[q - 1:q, :]
        xw = xs * (jnp.exp(cum_last - cum) * dtc)
        xdt = xs * dtc

        xcb = x_ref[0, rows, :].astype(BF16)
        dtt = _softplus(_dot_nt(w_dtt_ref[...], xcb) + dtb_c_ref[...])
        if masked:
            t_row = j * tl + r0 + _iota((1, q), 1)
            dtt = jnp.where(t_row < l_valid, dtt, 0.0)
        dat = dtt * a_c
        cum_t2 = _dot(dat[0:8, :], mt_left, HIGHEST) + _dot(dat[8:16, :], mt_right, HIGHEST)

        pieces = []
        for g in range(SSD_GROUPS):
            gs = slice(g * GROUP_CH, (g + 1) * GROUP_CH)
            bg = xa_buf[rows, 1024 + g * SSD_STATE:1024 + (g + 1) * SSD_STATE].astype(BF16)
            cg = xa_buf[rows, 1280 + g * SSD_STATE:1280 + (g + 1) * SSD_STATE].astype(BF16)
            cb2 = _dot_nt(cg, jnp.concatenate([bg, bg], axis=0))
            h_g = h_ref[:, gs]
            y_off = _dot(cg, h_g.astype(BF16)) * ecum[:, gs]
            for pi in range(GROUP_CH // LANES):
                i = g * (GROUP_CH // LANES) + pi
                ls = slice(i * LANES, (i + 1) * LANES)
                seg = cum[:, ls] - cum_t2[i:i + 1, :]
                mixp = (cb2 * jnp.exp(jnp.where(causal2, seg, -jnp.inf))).astype(BF16)
                xp = xdt[:, ls]
                wblk = jnp.concatenate([jnp.where(low_half, xp, 0.0), jnp.where(low_half, 0.0, xp)],
                                       axis=0).astype(BF16)
                y_pair = (_dot(mixp, wblk) + y_off[:, pi * LANES:(pi + 1) * LANES]
                          + xs[:, ls] * dskip[:, ls])
                pieces.append(y_pair)
            h_ref[:, gs] = h_g * ecum[q - 1:q, gs] + _dot_tn(bg, xw[:, gs].astype(BF16))

        for g in range(SSD_GROUPS):
            gs = slice(g * GROUP_CH, (g + 1) * GROUP_CH)
            yg = jnp.concatenate(pieces[g * 4:(g + 1) * 4], axis=1) * _silu(z_buf[rows, gs])
            ms = jnp.mean(yg * yg, axis=-1, keepdims=True)
            mix_buf[rows, gs] = (yg * lax.rsqrt(ms + EPS) * normw[:, gs]).astype(BF16)
        return carry

    lax.fori_loop(0, nq, chunk, 0, unroll=True)

    pos = (pos0 + j * tl + _iota((tl, 1), 0)).astype(F32)
    for gi, w in enumerate(POOL_WINDOWS):
        cs = slice(gi * POOL_GROUP_DIM, (gi + 1) * POOL_GROUP_DIM)
        cur = u_buf[:, cs]
        d = 1
        while d < w:
            cur = cur + pltpu.roll(cur, d, 0)
            d *= 2
        cnt = jnp.minimum(pos + 1.0, float(w))
        diff = cur[16:, :] / cnt - u_buf[16:16 + tl, cs]
        yb = _dot(diff.astype(BF16), w_pool_ref[gi]) * pscale_ref[:, cs]
        mix_buf[:, SSD_D_INNER + gi * POOL_GROUP_DIM:SSD_D_INNER + (gi + 1) * POOL_GROUP_DIM] = yb.astype(BF16)

    mix = _dot(mix_buf[...], w_out_ref[...])
    y_ref[0] = _layer_norm(ALPHA * x + mix, lng_ref[...], lnb_ref[...])

    lv = l_valid - (nt - 1) * tl

    @pl.when(j == nt - 1)
    def _():
        conv_o_ref[0] = xbc_buf[lv:lv + 8, :]
        pool_o_ref[0] = u_buf[lv:lv + 16, :]
        ssm_o_ref[0] = h_ref[...]

    @pl.when(j < nt - 1)
    def _():
        xbc_buf[0:8, :] = xbc_buf[tl:tl + 8, :]
        u_buf[0:16, :] = u_buf[tl:tl + 16, :]


def _const_spec(shape):
    nd = len(shape)
    return pl.BlockSpec(shape, lambda *_: (0,) * nd)


def _l0_mixer(x, conv0, ssm0_t, pool0, params, *, tl, l_valid, pos0):
    bsz, l_total, _ = x.shape
    nt = l_total // tl
    per_b3 = lambda b, j: (b, 0, 0)
    in_specs = [
        pl.BlockSpec((1, tl, D_MODEL), lambda b, j: (b, j, 0)),
        pl.BlockSpec((1, 8, SSD_CONV_DIM), per_b3),
        pl.BlockSpec((1, SSD_STATE, SSD_D_INNER), per_b3),
        pl.BlockSpec((1, 16, POOL_DIM), per_b3),
    ] + [_const_spec(p.shape) for p in params]
    out_shape = (
        jax.ShapeDtypeStruct((bsz, l_total, D_MODEL), F32),
        jax.ShapeDtypeStruct((bsz, 8, SSD_CONV_DIM), F32),
        jax.ShapeDtypeStruct((bsz, SSD_STATE, SSD_D_INNER), F32),
        jax.ShapeDtypeStruct((bsz, 16, POOL_DIM), F32),
    )
    out_specs = (
        pl.BlockSpec((1, tl, D_MODEL), lambda b, j: (b, j, 0)),
        pl.BlockSpec((1, 8, SSD_CONV_DIM), per_b3),
        pl.BlockSpec((1, SSD_STATE, SSD_D_INNER), per_b3),
        pl.BlockSpec((1, 16, POOL_DIM), per_b3),
    )
    scratch = [
        pltpu.VMEM((8 + tl, SSD_CONV_DIM), F32),
        pltpu.VMEM((16 + tl, POOL_DIM), F32),
        pltpu.VMEM((tl, SSD_CONV_DIM), F32),
        pltpu.VMEM((tl, SSD_D_INNER), F32),
        pltpu.VMEM((tl, SSD_D_INNER), F32),
        pltpu.VMEM((SSD_STATE, SSD_D_INNER), F32),
        pltpu.VMEM((tl, 2 * D_MODEL), BF16),
    ]
    return pl.pallas_call(
        functools.partial(_l0_kernel, tl=tl, l_total=l_total, l_valid=l_valid, pos0=pos0),
        grid=(bsz, nt),
        in_specs=in_specs,
        out_specs=out_specs,
        out_shape=out_shape,
        scratch_shapes=scratch,
        compiler_params=pltpu.CompilerParams(
            dimension_semantics=("arbitrary", "arbitrary"), vmem_limit_bytes=VMEM_LIMIT),
        name="l0_mixer",
    )(x, conv0, ssm0_t, pool0, *params)


def _l0_params(w_in, conv_w, conv_b, dt_bias, a_log, d_skip, ssd_norm, w_pool, pool_scale, w_out, ln_g, ln_b):
    rep = lambda v: jnp.repeat(v.astype(F32), SSD_HEAD_DIM)[None, :]
    z0, z1 = 0, SSD_D_INNER
    x1 = z1 + SSD_CONV_DIM
    d1 = x1 + SSD_HEADS
    w_z, w_xbc, w_dt, w_u = w_in[:, z0:z1], w_in[:, z1:x1], w_in[:, x1:d1], w_in[:, d1:]
    w_dtx = jnp.repeat(w_dt, SSD_HEAD_DIM, axis=1)
    w_all = jnp.concatenate([w_z, w_xbc, w_u, w_dtx], axis=1).astype(BF16)
    perm = jnp.concatenate([jnp.arange(0, SSD_HEADS, 2), jnp.arange(1, SSD_HEADS, 2)])
    a = -jnp.exp(a_log.astype(F32))
    return (
        w_all,
        w_dt.T[perm].astype(BF16),
        conv_w.astype(F32), conv_b.astype(F32)[None, :],
        rep(dt_bias), dt_bias.astype(F32)[perm][:, None],
        rep(a), a[perm][:, None],
        rep(d_skip), ssd_norm.astype(F32)[None, :],
        w_pool.astype(BF16), pool_scale.astype(F32)[None, :],
        w_out.astype(BF16), ln_g.astype(F32)[None, :], ln_b.astype(F32)[None, :],
    )


def _l1_kernel(x_ref, st0_ref, w_in_ref, lb_ref, normw_ref, w_out_ref, lng_ref, lnb_ref,
               y_ref, st_o_ref,
               q_buf, k_buf, lf_buf, v_buf, gt_buf, g_buf, s_ref, mix_buf,
               *, tl, l_total, l_valid):
    j = pl.program_id(1)
    nt = l_total // tl
    c = HGRN_C
    nc = tl // c
    nsub = c // HGRN_SUB
    masked = l_valid < l_total

    @pl.when(j == 0)
    def _():
        s_ref[...] = st0_ref[0]

    x = x_ref[0]
    xb = x.astype(BF16)
    lb = lb_ref[...]
    q_raw = _dot(xb, w_in_ref[:, 0:1024])
    fr = _dot(xb, w_in_ref[:, 1024:2048])
    v_buf[...] = _dot(xb, w_in_ref[:, 2048:3072])
    gt_buf[...] = _silu(_dot(xb, w_in_ref[:, 3072:4096]))

    e = jnp.exp(-jnp.abs(fr))
    r = 1.0 / (1.0 + e)
    sig_pos = jnp.where(fr >= 0.0, r, e * r)
    sig_neg = jnp.where(fr >= 0.0, e * r, r)
    f = lb + (1.0 - lb) * sig_pos
    log_f = jnp.where(f > 0.0, jnp.log(f), jnp.log1p(-lb) + jnp.minimum(fr, 0.0))
    kk = (1.0 - lb) * sig_neg
    if masked:
        ok = (j * tl + _iota((tl, 1), 0)) < l_valid
        log_f = jnp.where(ok, log_f, 0.0)
        kk = jnp.where(ok, kk, 0.0)
    lf_buf[...] = log_f
    k_buf[...] = kk
    q_buf[...] = _silu(q_raw)

    ltri = (_iota((c, c), 1) <= _iota((c, c), 0)).astype(F32)
    s_col = _iota((HGRN_SUB, c), 1)
    l_row = _iota((HGRN_SUB, c), 0)
    normw = normw_ref[...]

    def chunk(ci, carry):
        r0 = pl.multiple_of(ci * c, c)
        rows = pl.ds(r0, c)
        g_all = _dot(ltri, lf_buf[rows, :], HIGHEST)
        g_buf[...] = g_all
        g_last = g_all[c - 1:c, :]
        q_c = q_buf[rows, :]
        k_c = k_buf[rows, :]
        qt = q_c * jnp.exp(g_all)
        kd = k_c * jnp.exp(g_last - g_all)
        for h in range(HGRN_HEADS):
            hs = slice(h * HGRN_DK, (h + 1) * HGRN_DK)
            st = s_ref[h]
            v_h = v_buf[rows, hs].astype(BF16)
            o_h = _dot_nt(qt[:, hs].astype(BF16), st.astype(BF16))
            sc_rows = []
            for si in range(nsub):
                lo, hi = si * HGRN_SUB, (si + 1) * HGRN_SUB
                g_sub = g_buf[lo:hi, hs]
                if si == 0:
                    a_i = q_c[lo:hi, hs] * jnp.exp(g_sub)
                    b_i = k_c[0:hi, hs] * jnp.exp(-g_buf[0:hi, hs])
                else:
                    g_ref_row = g_buf[lo - 1:lo, hs]
                    a_i = q_c[lo:hi, hs] * jnp.exp(g_sub - g_ref_row)
                    b_i = k_c[0:hi, hs] * jnp.exp(g_ref_row - g_buf[0:hi, hs])
                b_i = b_i.astype(BF16)
                if hi < c:
                    b_i = jnp.concatenate([b_i, jnp.zeros((c - hi, HGRN_DK), BF16)], axis=0)
                sc = _dot_nt(a_i.astype(BF16), b_i)
                sc_rows.append(jnp.where(s_col <= l_row + lo, sc, 0.0))
            scores = jnp.concatenate(sc_rows, axis=0).astype(BF16)
            o_h = o_h + _dot(scores, v_h)
            s_ref[h] = st * jnp.exp(g_last[:, hs]) + _dot_tn(v_h, kd[:, hs].astype(BF16))
            ms = jnp.mean(o_h * o_h, axis=-1, keepdims=True)
            mix_buf[rows, hs] = (o_h * lax.rsqrt(ms + EPS) * normw * gt_buf[rows, hs]).astype(BF16)
        return carry

    lax.fori_loop(0, nc, chunk, 0, unroll=True)

    mix = _dot(mix_buf[...], w_out_ref[...])
    y_ref[0] = _layer_norm(ALPHA * x + mix, lng_ref[...], lnb_ref[...])

    @pl.when(j == nt - 1)
    def _():
        st_o_ref[0] = s_ref[...]


def _l1_mixer(x, st0_t, params, *, tl, l_valid):
    bsz, l_total, _ = x.shape
    nt = l_total // tl
    st_spec = pl.BlockSpec((1, HGRN_HEADS, HGRN_DV, HGRN_DK), lambda b, j: (b, 0, 0, 0))
    in_specs = [pl.BlockSpec((1, tl, D_MODEL), lambda b, j: (b, j, 0)), st_spec]
    in_specs += [_const_spec(p.shape) for p in params]
    big = lambda: pltpu.VMEM((tl, D_MODEL), F32)
    return pl.pallas_call(
        functools.partial(_l1_kernel, tl=tl, l_total=l_total, l_valid=l_valid),
        grid=(bsz, nt),
        in_specs=in_specs,
        out_specs=(pl.BlockSpec((1, tl, D_MODEL), lambda b, j: (b, j, 0)), st_spec),
        out_shape=(jax.ShapeDtypeStruct((bsz, l_total, D_MODEL), F32),
                   jax.ShapeDtypeStruct((bsz, HGRN_HEADS, HGRN_DV, HGRN_DK), F32)),
        scratch_shapes=[big(), big(), big(), big(), big(),
                        pltpu.VMEM((HGRN_C, D_MODEL), F32),
                        pltpu.VMEM((HGRN_HEADS, HGRN_DV, HGRN_DK), F32),
                        pltpu.VMEM((tl, D_MODEL), BF16)],
        compiler_params=pltpu.CompilerParams(
            dimension_semantics=("arbitrary", "arbitrary"), vmem_limit_bytes=VMEM_LIMIT),
        name="l1_mixer",
    )(x, st0_t, *params)


def _l1_params(w_in, lb, hgrn_norm, w_out, ln_g, ln_b):
    return (w_in.astype(BF16), lb.astype(F32)[None, :], hgrn_norm.astype(F32)[None, :],
            w_out.astype(BF16), ln_g.astype(F32)[None, :], ln_b.astype(F32)[None, :])


def _route_kernel(x_ref, wrt_ref, bias_ref, idx_ref, gate_ref, pos_ref, cnt_ref, count_sc, *, tt):
    i = pl.program_id(0)

    @pl.when(i == 0)
    def _():
        count_sc[...] = jnp.zeros_like(count_sc)

    neg = -jnp.inf
    scores = jax.nn.sigmoid(_dot_nt(wrt_ref[...], x_ref[...], HIGHEST))
    sel3 = (scores + bias_ref[...]).reshape(N_EXPERT_GROUPS, GROUP_SIZE, tt)

    im = _iota(sel3.shape, 1)
    m1 = jnp.max(sel3, axis=1, keepdims=True)
    i1 = jnp.min(jnp.where(sel3 == m1, im, GROUP_SIZE), axis=1, keepdims=True)
    m2 = jnp.max(jnp.where(im == i1, neg, sel3), axis=1, keepdims=True)
    grp = (m1 + m2).reshape(N_EXPERT_GROUPS, tt)

    ig = _iota(grp.shape, 0)
    keep = jnp.zeros(grp.shape, F32)
    for _ in range(TOPK_GROUPS):
        gm = jnp.max(grp, axis=0, keepdims=True)
        hit = ig == jnp.min(jnp.where(grp == gm, ig, N_EXPERT_GROUPS), axis=0, keepdims=True)
        keep = jnp.where(hit, 1.0, keep)
        grp = jnp.where(hit, neg, grp)

    selm = jnp.where(keep.reshape(N_EXPERT_GROUPS, 1, tt) > 0.0, sel3, neg).reshape(N_EXPERTS, tt)
    ie = _iota(selm.shape, 0)
    membf = jnp.zeros(selm.shape, F32)
    idx_rows, w_rows = [], []
    for _ in range(TOP_K):
        m = jnp.max(selm, axis=0, keepdims=True)
        ei = jnp.min(jnp.where(selm == m, ie, N_EXPERTS), axis=0, keepdims=True)
        hit = ie == ei
        idx_rows.append(ei)
        w_rows.append(jnp.sum(jnp.where(hit, scores, 0.0), axis=0, keepdims=True))
        selm = jnp.where(hit, neg, selm)
        membf = jnp.where(hit, 1.0, membf)
    idx = jnp.concatenate(idx_rows, axis=0)
    w = jnp.concatenate(w_rows, axis=0)
    gate_ref[...] = w / jnp.sum(w, axis=0, keepdims=True) * ROUTED_SCALE
    idx_ref[...] = idx

    before =(_iota((tt, tt), 0) < _iota((tt, tt), 1)).astype(BF16)
    pos_all = _dot(membf.astype(BF16), before) + count_sc[:, 0:1]
    pos_rows = [jnp.sum(jnp.where(ie == idx_rows[k], pos_all, 0.0), axis=0, keepdims=True)
                for k in range(TOP_K)]
    pos_ref[...] = jnp.concatenate(pos_rows, axis=0).astype(I32)
    count_sc[...] = count_sc[...] + jnp.sum(membf, axis=1, keepdims=True)
    cnt_ref[...] = count_sc[...].astype(I32)


def _route(x2, w_r, b_r, *, tt):
    t_total = x2.shape[0]
    row_spec = pl.BlockSpec((TOP_K, tt), lambda i: (0, i))
    return pl.pallas_call(
        functools.partial(_route_kernel, tt=tt),
        grid=(t_total // tt,),
        in_specs=[pl.BlockSpec((tt, D_MODEL), lambda i: (i, 0)),
                  _const_spec((N_EXPERTS, D_MODEL)), _const_spec((N_EXPERTS, 1))],
        out_specs=(row_spec, row_spec, row_spec, _const_spec((N_EXPERTS, LANES))),
        out_shape=(jax.ShapeDtypeStruct((TOP_K, t_total), I32),
                   jax.ShapeDtypeStruct((TOP_K, t_total), F32),
                   jax.ShapeDtypeStruct((TOP_K, t_total), I32),
                   jax.ShapeDtypeStruct((N_EXPERTS, LANES), I32)),
        scratch_shapes=[pltpu.VMEM((N_EXPERTS, LANES), F32)],
        compiler_params=pltpu.CompilerParams(dimension_semantics=("arbitrary",), vmem_limit_bytes=VMEM_LIMIT),
        name="moe_route",
    )(x2, w_r.astype(F32).T, b_r.astype(F32)[:, None])


HIGH_HALF = 0xFFFF0000


def _pack_rows(dst_ref, val):
    n = val.shape[0]
    bits = lax.bitcast_convert_type(val.astype(BF16).astype(F32), U32)
    words = (bits[:, HALF_D:] & jnp.uint32(HIGH_HALF)) | (bits[:, :HALF_D] >> 16)
    for jj in range(PACK_ROWS):
        dst_ref[pl.ds(jj, n, stride=PACK_ROWS), :] = words[:, jj * LANES:(jj + 1) * LANES]


def _unpack_rows(src_ref):
    n = src_ref.shape[0] // PACK_ROWS
    words = [src_ref[pl.ds(jj, n, stride=PACK_ROWS), :] for jj in range(PACK_ROWS)]
    low = [lax.bitcast_convert_type(w << 16, F32) for w in words]
    high = [lax.bitcast_convert_type(w & jnp.uint32(HIGH_HALF), F32) for w in words]
    return jnp.concatenate(low + high, axis=1)


def _packed_row(r):
    return pl.ds(pl.multiple_of(r * PACK_ROWS, PACK_ROWS), PACK_ROWS)


def _dispatch_kernel(dest_ref, x_ref, xs_hbm, xp, sem, *, tt):
    _pack_rows(xp, x_ref[...])

    def issue(t, carry):
        for k in range(TOP_K):
            pltpu.make_async_copy(xp.at[_packed_row(t)], xs_hbm.at[_packed_row(dest_ref[t * TOP_K + k])],
                                  sem).start(priority=k % 2)
        return carry

    lax.fori_loop(0, tt, issue, 0)
    for _ in range(TOP_K):
        pltpu.make_async_copy(xp, xs_hbm.at[pl.ds(0, tt * PACK_ROWS)], sem).wait()


def _dispatch(x2, dest_flat, n_rows, *, tt):
    t_total = x2.shape[0]
    return pl.pallas_call(
        functools.partial(_dispatch_kernel, tt=tt),
        grid=(t_total // tt,),
        in_specs=[pl.BlockSpec((tt * TOP_K,), lambda i: (i,), memory_space=pltpu.SMEM),
                  pl.BlockSpec((tt, D_MODEL), lambda i: (i, 0))],
        out_specs=pl.BlockSpec(memory_space=pl.ANY),
        out_shape=jax.ShapeDtypeStruct((n_rows * PACK_ROWS, LANES), U32),
        scratch_shapes=[pltpu.VMEM((tt * PACK_ROWS, LANES), U32), pltpu.SemaphoreType.DMA(())],
        compiler_params=pltpu.CompilerParams(dimension_semantics=("arbitrary",), vmem_limit_bytes=VMEM_LIMIT),
        name="moe_dispatch",
    )(dest_flat, x2)


def _experts_kernel(blk_e_ref, blk_rows_ref, n_used_ref, xs_ref, w_in_ref, w_out_ref, ys_ref,
                    w_in_bf, w_out_bf, *, bm):
    i = pl.program_id(0)

    @pl.when(i < n_used_ref[0])
    def _():
        @pl.when((i == 0) | (blk_e_ref[i] != blk_e_ref[jnp.maximum(i - 1, 0)]))
        def _():
            w_in_bf[...] = w_in_ref[0, 0].astype(BF16)
            w_out_bf[...] = w_out_ref[0, 0].astype(BF16)

        live = _iota((bm, 1), 0) < blk_rows_ref[i]
        xb = jnp.where(live, _unpack_rows(xs_ref), 0.0).astype(BF16)
        h = _dot(xb, w_in_bf[...])
        act = (_silu(h[:, 0:EXPERT_FF]) * h[:, EXPERT_FF:2 * EXPERT_FF]).astype(BF16)
        _pack_rows(ys_ref, _dot(act, w_out_bf[...]))


def _experts(xs, blk_e, blk_rows, n_used, w_in, w_out, layer, *, bm):
    n_rows = xs.shape[0] // PACK_ROWS
    n_blocks = n_rows // bm
    last = lambda i, be, br, nu: jnp.minimum(i, nu[0] - 1)
    row_spec = pl.BlockSpec((bm * PACK_ROWS, LANES), lambda i, be, br, nu: (last(i, be, br, nu), 0))
    grid_spec = pltpu.PrefetchScalarGridSpec(
        num_scalar_prefetch=3,
        grid=(n_blocks,),
        in_specs=[row_spec,
                  pl.BlockSpec((1, 1, D_MODEL, 2 * EXPERT_FF),
                               lambda i, be, br, nu: (layer, be[last(i, be, br, nu)], 0, 0)),
                  pl.BlockSpec((1, 1, EXPERT_FF, D_MODEL),
                               lambda i, be, br, nu: (layer, be[last(i, be, br, nu)], 0, 0))],
        out_specs=row_spec,
        scratch_shapes=[pltpu.VMEM((D_MODEL, 2 * EXPERT_FF), BF16), pltpu.VMEM((EXPERT_FF, D_MODEL), BF16)],
    )
    return pl.pallas_call(
        functools.partial(_experts_kernel, bm=bm),
        grid_spec=grid_spec,
        out_shape=jax.ShapeDtypeStruct((n_rows * PACK_ROWS, LANES), U32),
        compiler_params=pltpu.CompilerParams(dimension_semantics=("arbitrary",), vmem_limit_bytes=VMEM_LIMIT),
        name="moe_experts",
    )(blk_e, blk_rows, n_used, xs, w_in, w_out)


def _combine_kernel(dest_ref, dest_next_ref, x_ref, gate_ref, ys_hbm, ws_in_ref, ws_out_ref, lng_ref, lnb_ref,
                    y_ref, buf_even, buf_odd, shared_buf, sems, *, tt, n_tiles):
    i = pl.program_id(0)
    grp = min(COMBINE_GROUP, tt)
    n_groups = tt // grp

    def request_rows(d_ref, buf, sem, t0):
        for tj in range(grp):
            for k in range(TOP_K):
                pltpu.make_async_copy(ys_hbm.at[_packed_row(d_ref[(t0 + tj) * TOP_K + k])],
                                      buf.at[k, _packed_row(t0 + tj)],
                                      sem).start(priority=k % 2)

    def combine_rows(buf, t0):
        rows = pl.ds(t0, grp)
        gates = gate_ref[rows, :]
        packed = pl.ds(pl.multiple_of(t0 * PACK_ROWS, grp * PACK_ROWS), grp * PACK_ROWS)
        routed = _unpack_rows(buf.at[0, packed]) * gates[:, 0:1]
        for k in range(1, TOP_K):
            routed = routed + _unpack_rows(buf.at[k, packed]) * gates[:, k:k + 1]
        y_ref[rows, :] = _layer_norm(ALPHA * x_ref[rows, :] + (routed + shared_buf[rows, :]),
                                     lng_ref[...], lnb_ref[...])

    def first_requests(g, carry):
        request_rows(dest_ref, buf_even, sems.at[0], pl.multiple_of(g * grp, grp))
        return carry

    @pl.when(i == 0)
    def _():
        lax.fori_loop(0, n_groups, first_requests, 0)

    h = _dot(x_ref[...].astype(BF16), ws_in_ref[...])
    act = (_silu(h[:, 0:EXPERT_FF]) * h[:, EXPERT_FF:2 * EXPERT_FF]).astype(BF16)
    shared_buf[...] = _dot(act, ws_out_ref[...])

    def step(cur, cur_sem, nxt, nxt_sem):
        for k in range(TOP_K):
            pltpu.make_async_copy(ys_hbm.at[pl.ds(0, tt * PACK_ROWS)], cur.at[k], cur_sem).wait()

        def requests_and_combine(g, carry):
            t0 = pl.multiple_of(g * grp, grp)
            request_rows(dest_next_ref, nxt, nxt_sem, t0)
            combine_rows(cur, t0)
            return carry

        def combine_only(g, carry):
            combine_rows(cur, pl.multiple_of(g * grp, grp))
            return carry

        @pl.when(i + 1 < n_tiles)
        def _():
            lax.fori_loop(0, n_groups, requests_and_combine, 0)

        @pl.when(i + 1 >= n_tiles)
        def _():
            lax.fori_loop(0, n_groups, combine_only, 0)

    @pl.when(i % 2 == 0)
    def _():
        step(buf_even, sems.at[0], buf_odd, sems.at[1])

    @pl.when(i % 2 == 1)
    def _():
        step(buf_odd, sems.at[1], buf_even, sems.at[0])


def _combine(x2, dest_flat, gates, ys, ws_in, ws_out, ln_g, ln_b, *, tt):
    t_total = x2.shape[0]
    n_tiles = t_total // tt
    return pl.pallas_call(
        functools.partial(_combine_kernel, tt=tt, n_tiles=n_tiles),
        grid=(n_tiles,),
        in_specs=[pl.BlockSpec((tt * TOP_K,), lambda i: (i,), memory_space=pltpu.SMEM),
                  pl.BlockSpec((tt * TOP_K,), lambda i: (jnp.minimum(i + 1, n_tiles - 1),), memory_space=pltpu.SMEM),
                  pl.BlockSpec((tt, D_MODEL), lambda i: (i, 0)),
                  pl.BlockSpec((tt, TOP_K), lambda i: (i, 0)),
                  pl.BlockSpec(memory_space=pl.ANY),
                  _const_spec((D_MODEL, 2 * EXPERT_FF)), _const_spec((EXPERT_FF, D_MODEL)),
                  _const_spec((1, D_MODEL)), _const_spec((1, D_MODEL))],
        out_specs=pl.BlockSpec((tt, D_MODEL), lambda i: (i, 0)),
        out_shape=jax.ShapeDtypeStruct((t_total, D_MODEL), F32),
        scratch_shapes=[pltpu.VMEM((TOP_K, tt * PACK_ROWS, LANES), U32), pltpu.VMEM((TOP_K, tt * PACK_ROWS, LANES), U32),
                        pltpu.VMEM((tt, D_MODEL), F32), pltpu.SemaphoreType.DMA((2,))],
        compiler_params=pltpu.CompilerParams(dimension_semantics=("arbitrary",), vmem_limit_bytes=VMEM_LIMIT),
        name="moe_combine",
    )(dest_flat, dest_flat, x2, gates, ys, ws_in.astype(BF16), ws_out.astype(BF16),
      ln_g.astype(F32)[None, :], ln_b.astype(F32)[None, :])


def _moe_ffn(x, w_r, b_r, w_in, w_out, layer, ws_in, ws_out, ln_g, ln_b, *, tt, bm):
    bsz, l_total, _ = x.shape
    t_total = bsz * l_total
    x2 = x.reshape(t_total, D_MODEL)
    idx_t, gate_t, pos_t, cnt = _route(x2, w_r, b_r, tt=min(ROUTE_TILE, t_total))

    counts = cnt[:, 0]
    padded = (counts + bm - 1) // bm * bm
    pad_end = jnp.cumsum(padded)
    pad_start = pad_end - padded
    onehot = idx_t[:, :, None] == jnp.arange(N_EXPERTS, dtype=I32)
    dest_t = pos_t + jnp.sum(jnp.where(onehot, pad_start, 0), axis=-1)
    dest_flat = dest_t.T.reshape(-1)
    n_blocks = -(-(t_total * TOP_K + N_EXPERTS * (bm - 1)) // bm)
    blk_start = jnp.arange(n_blocks, dtype=I32) * bm
    blk_e = jnp.sum((pad_end[None, :] <= blk_start[:, None]).astype(I32), axis=1)
    blk_e = jnp.minimum(blk_e, N_EXPERTS - 1)
    blk_rows = jnp.clip(pad_start[blk_e] + counts[blk_e] - blk_start, 0, bm).astype(I32)
    n_used = (pad_end[-1:] // bm).astype(I32)

    tt_d = min(tt, t_total)
    xs = _dispatch(x2, dest_flat, n_blocks * bm, tt=tt_d)
    ys = _experts(xs, blk_e, blk_rows, n_used, w_in.astype(F32), w_out.astype(F32), layer, bm=bm)
    y = _combine(x2, dest_flat, gate_t.T, ys, ws_in, ws_out, ln_g, ln_b, tt=tt_d)
    return y.reshape(bsz, l_total, D_MODEL)


def _pad_time(t, total):
    return jnp.pad(t, [(0, 0), (0, total - t.shape[1]), (0, 0)])


def _trunk(x, pos0, conv_st, ssm_st, pool_st, hgrn_st, weights, *, tl, tt, bm):
    (w_in_l0, conv_w_l0, conv_b_l0, dt_bias_l0, a_log_l0, d_skip_l0, ssd_norm_l0, w_pool_l0,
     pool_scale_l0, w_out_l0, w_in_l1, lb_raw, hgrn_norm_l1, w_out_l1, ln_mix_g, ln_mix_b,
     ln_ffn_g, ln_ffn_b, router_w, router_bias, moe_w_in, moe_w_out, shared_w_in, shared_w_out) = weights
    bsz, l_valid, _ = x.shape
    l_total = -(-l_valid // tl) * tl
    xp = _pad_time(x, l_total)

    lb_sm = jax.nn.softmax(lb_raw.astype(F32), axis=0)
    lb_all = jnp.cumsum(lb_sm, axis=0) - lb_sm[0]

    conv0 = jnp.pad(conv_st.astype(F32), [(0, 0), (8 - (SSD_CONV_W - 1), 0), (0, 0)])
    pool0 = jnp.pad(pool_st.astype(F32), [(0, 0), (1, 0), (0, 0)])
    ssm0_t = ssm_st.astype(F32).reshape(bsz, SSD_D_INNER, SSD_STATE).transpose(0, 2, 1)
    p0 = _l0_params(w_in_l0, conv_w_l0, conv_b_l0, dt_bias_l0, a_log_l0, d_skip_l0, ssd_norm_l0,
                    w_pool_l0, pool_scale_l0, w_out_l0, ln_mix_g[0], ln_mix_b[0])
    h, conv_o, ssm_o, pool_o = _l0_mixer(xp, conv0, ssm0_t, pool0, p0, tl=tl, l_valid=l_valid, pos0=pos0)
    new_conv = conv_o[:, 8 - (SSD_CONV_W - 1):]
    new_pool = pool_o[:, 1:]
    new_ssm = ssm_o.transpose(0, 2, 1).reshape(bsz, SSD_HEADS, SSD_HEAD_DIM, SSD_STATE)
    h = _moe_ffn(h[:, :l_valid], router_w[0], router_bias[0], moe_w_in, moe_w_out, 0, shared_w_in[0],
                 shared_w_out[0], ln_ffn_g[0], ln_ffn_b[0], tt=tt, bm=bm)

    p1 = _l1_params(w_in_l1, lb_all[1], hgrn_norm_l1, w_out_l1, ln_mix_g[1], ln_mix_b[1])
    h, hgrn_o = _l1_mixer(_pad_time(h, l_total), hgrn_st.astype(F32).transpose(0, 1, 3, 2), p1,
                          tl=tl, l_valid=l_valid)
    new_hgrn = hgrn_o.transpose(0, 1, 3, 2)
    h = _moe_ffn(h[:, :l_valid], router_w[1], router_bias[1], moe_w_in, moe_w_out, 1, shared_w_in[1],
                 shared_w_out[1], ln_ffn_g[1], ln_ffn_b[1], tt=tt, bm=bm)
    return h, new_conv, new_ssm, new_pool, new_hgrn


PROMPT_TL = 256
SAMPLE_TL = 128
TOKEN_TILE = 256
EXPERT_BLOCK = 1024


def kernel(x_prompt, x_sample, state_conv_l0, state_ssm_l0, state_pool_l0, state_hgrn_l1, w_in_l0, conv_w_l0, conv_b_l0, dt_bias_l0, a_log_l0, d_skip_l0, ssd_norm_l0, w_pool_l0, pool_scale_l0, w_out_l0, w_in_l1, lb_raw, hgrn_norm_l1, w_out_l1, ln_mix_g, ln_mix_b, ln_ffn_g, ln_ffn_b, router_w, router_bias, moe_w_in, moe_w_out, shared_w_in, shared_w_out):
    weights = (w_in_l0, conv_w_l0, conv_b_l0, dt_bias_l0, a_log_l0, d_skip_l0, ssd_norm_l0, w_pool_l0,
               pool_scale_l0, w_out_l0, w_in_l1, lb_raw, hgrn_norm_l1, w_out_l1, ln_mix_g, ln_mix_b,
               ln_ffn_g, ln_ffn_b, router_w, router_bias, moe_w_in, moe_w_out, shared_w_in, shared_w_out)
    bp = x_prompt.shape[0]
    past_len = 1024
    zeros = lambda *s: jnp.zeros(s, F32)
    y_p, p_conv, p_ssm, p_pool, p_hgrn = _trunk(
        x_prompt, 0, zeros(bp, SSD_CONV_W - 1, SSD_CONV_DIM), zeros(bp, SSD_HEADS, SSD_HEAD_DIM, SSD_STATE),
        zeros(bp, POOL_MAXW - 1, POOL_DIM), zeros(bp, HGRN_HEADS, HGRN_DK, HGRN_DV), weights,
        tl=PROMPT_TL, tt=TOKEN_TILE, bm=EXPERT_BLOCK)
    y_s, s_conv, s_ssm, s_pool, s_hgrn = _trunk(
        x_sample, past_len, state_conv_l0, state_ssm_l0, state_pool_l0, state_hgrn_l1, weights,
        tl=SAMPLE_TL, tt=TOKEN_TILE, bm=EXPERT_BLOCK)
    return (y_p, y_s, p_conv, p_ssm, p_pool, p_hgrn, s_conv, s_ssm, s_pool, s_hgrn)
```

```python
import functools
import math

import jax
import jax.numpy as jnp
from jax import lax
from jax.experimental import pallas as pl
from jax.experimental.pallas import tpu as pltpu

F32 = jnp.float32
BF16 = jnp.bfloat16
I32 = jnp.int32
U32 = jnp.uint32
HIGHEST = lax.Precision.HIGHEST

D_MODEL = 1024
DEPTH = 2
ALPHA = (2 * DEPTH) ** 0.25
EPS = 1e-5

SSD_HEADS = 16
SSD_HEAD_DIM = 64
SSD_GROUPS = 2
SSD_STATE = 128
SSD_D_INNER = 1024
SSD_CONV_DIM = 1536
SSD_CONV_W = 4
SSD_Q = 64
GROUP_CH = SSD_D_INNER // SSD_GROUPS

POOL_DIM = 1024
POOL_WINDOWS = (2, 4, 8, 16)
POOL_GROUP_DIM = 256
POOL_MAXW = 16

HGRN_HEADS = 8
HGRN_DK = 128
HGRN_DV = 128
HGRN_C = 128
HGRN_SUB = 16
HGRN_MAX_SPAN = 80.0

N_EXPERTS = 64
N_EXPERT_GROUPS = 8
GROUP_SIZE = N_EXPERTS // N_EXPERT_GROUPS
TOPK_GROUPS = 4
TOP_K = 8
EXPERT_FF = 256
ROUTED_SCALE = 2.5
ROUTE_TILE = 512
COMBINE_GROUP = 32

V7X_VMEM_BYTES = 64 * 1024 * 1024
VMEM_LIMIT = V7X_VMEM_BYTES - 8 * 1024 * 1024
LANES = 128
SUBLANES = 8
HALF_D = D_MODEL // 2
PACK_ROWS = HALF_D // LANES

NT_DIMS = (((1,), (1,)), ((), ()))
TN_DIMS = (((0,), (0,)), ((), ()))


def _dot(a, b, precision=None):
    return jnp.dot(a, b, preferred_element_type=F32, precision=precision)


def _dot_nt(a, b, precision=None):
    return lax.dot_general(a, b, NT_DIMS, preferred_element_type=F32, precision=precision)


def _dot_tn(a, b):
    return lax.dot_general(a, b, TN_DIMS, preferred_element_type=F32)


def _softplus(v):
    return jnp.maximum(v, 0.0) + jnp.log1p(jnp.exp(-jnp.abs(v)))


def _silu(v):
    return v * jax.nn.sigmoid(v)


def _layer_norm(h, g, b):
    mu = jnp.mean(h, axis=-1, keepdims=True)
    hc = h - mu
    var = jnp.mean(hc * hc, axis=-1, keepdims=True)
    return hc * lax.rsqrt(var + EPS) * g + b


def _iota(shape, dim):
    return lax.broadcasted_iota(I32, shape, dim)


def _l0_kernel(x_ref, conv0_ref, ssm0_ref, pool0_ref,
               w_in_ref, w_dtt_ref, conv_w_ref, conv_b_ref, dtb_x_ref, dtb_c_ref, a_x_ref, a_c_ref,
               dskip_ref, normw_ref, w_pool_ref, pscale_ref, w_out_ref, lng_ref, lnb_ref,
               y_ref, conv_o_ref, ssm_o_ref, pool_o_ref,
               xbc_buf, u_buf, xa_buf, z_buf, dtx_buf, h_ref, mix_buf,
               *, tl, l_total, l_valid, pos0):
    j = pl.program_id(1)
    nt = l_total // tl
    q = SSD_Q
    nq = tl // q
    masked = l_valid < l_total

    @pl.when(j == 0)
    def _():
        xbc_buf[0:8, :] = conv0_ref[0]
        u_buf[0:16, :] = pool0_ref[0]
        h_ref[...] = ssm0_ref[0]

    x = x_ref[0]
    xb = x.astype(BF16)

    z_buf[...] = _dot(xb, w_in_ref[:, 0:1024])
    xbc_buf[8:8 + tl, :] = _dot(xb, w_in_ref[:, 1024:2560])
    u_buf[16:16 + tl, :] = _dot(xb, w_in_ref[:, 2560:3584])
    dtx = _softplus(_dot(xb, w_in_ref[:, 3584:4608]) + dtb_x_ref[...])
    if masked:
        t_col = j * tl + _iota((tl, 1), 0)
        dtx = jnp.where(t_col < l_valid, dtx, 0.0)
    dtx_buf[...] = dtx

    acc = conv_b_ref[...] + conv_w_ref[3:4, :] * xbc_buf[8:8 + tl, :]
    for jj in range(SSD_CONV_W - 1):
        acc = acc + conv_w_ref[jj:jj + 1, :] * xbc_buf[5 + jj:5 + jj + tl, :]
    xa_buf[...] = _silu(acc)

    a_x = a_x_ref[...]
    a_c = a_c_ref[...]
    dskip = dskip_ref[...]
    normw = normw_ref[...]

    ltri = (_iota((q, q), 1) <= _iota((q, q), 0)).astype(F32)
    sp = _iota((q, LANES), 0)
    ln = _iota((q, LANES), 1)
    mt_left = ((ln < q) & (sp <= ln)).astype(F32)
    mt_right = ((ln >= q) & (sp <= ln - q)).astype(F32)
    causal2 = (ln % q) <= sp
    low_half = ln < q

    def chunk(c, carry):
        r0 = pl.multiple_of(c * q, q)
        rows = pl.ds(r0, q)
        xs = xa_buf[rows, 0:1024]
        dtc = dtx_buf[rows, :]
        cum = _dot(ltri, dtc * a_x, HIGHEST)
        ecum = jnp.exp(cum)
        cum_last = cum[q - 1:q, :]
        xw = xs * (jnp.exp(cum_last - cum) * dtc)
        xdt = xs * dtc

        xcb = x_ref[0, rows, :].astype(BF16)
        dtt = _softplus(_dot_nt(w_dtt_ref[...], xcb) + dtb_c_ref[...])
        if masked:
            t_row = j * tl + r0 + _iota((1, q), 1)
            dtt = jnp.where(t_row < l_valid, dtt, 0.0)
        dat = dtt * a_c
        cum_t2 = _dot(dat[0:8, :], mt_left, HIGHEST) + _dot(dat[8:16, :], mt_right, HIGHEST)

        pieces = []
        for g in range(SSD_GROUPS):
            gs = slice(g * GROUP_CH, (g + 1) * GROUP_CH)
            bg = xa_buf[rows, 1024 + g * SSD_STATE:1024 + (g + 1) * SSD_STATE].astype(BF16)
            cg = xa_buf[rows, 1280 + g * SSD_STATE:1280 + (g + 1) * SSD_STATE].astype(BF16)
            cb2 = _dot_nt(cg, jnp.concatenate([bg, bg], axis=0))
            h_g = h_ref[:, gs]
            y_off = _dot(cg, h_g.astype(BF16)) * ecum[:, gs]
            for pi in range(GROUP_CH // LANES):
                i = g * (GROUP_CH // LANES) + pi
                ls = slice(i * LANES, (i + 1) * LANES)
                seg = cum[:, ls] - cum_t2[i:i + 1, :]
                mixp = (cb2 * jnp.exp(jnp.where(causal2, seg, -jnp.inf))).astype(BF16)
                xp = xdt[:, ls]
                wblk = jnp.concatenate([jnp.where(low_half, xp, 0.0), jnp.where(low_half, 0.0, xp)],
                                       axis=0).astype(BF16)
                y_pair = (_dot(mixp, wblk) + y_off[:, pi * LANES:(pi + 1) * LANES]
                          + xs[:, ls] * dskip[:, ls])
                pieces.append(y_pair)
            h_ref[:, gs] = h_g * ecum[q - 1:q, gs] + _dot_tn(bg, xw[:, gs].astype(BF16))

        for g in range(SSD_GROUPS):
            gs = slice(g * GROUP_CH, (g + 1) * GROUP_CH)
            yg = jnp.concatenate(pieces[g * 4:(g + 1) * 4], axis=1) * _silu(z_buf[rows, gs])
            ms = jnp.mean(yg * yg, axis=-1, keepdims=True)
            mix_buf[rows, gs] = (yg * lax.rsqrt(ms + EPS) * normw[:, gs]).astype(BF16)
        return carry

    lax.fori_loop(0, nq, chunk, 0, unroll=True)

    pos = (pos0 + j * tl + _iota((tl, 1), 0)).astype(F32)
    for gi, w in enumerate(POOL_WINDOWS):
        cs = slice(gi * POOL_GROUP_DIM, (gi + 1) * POOL_GROUP_DIM)
        cur = u_buf[:, cs]
        d = 1
        while d < w:
            cur = cur + pltpu.roll(cur, d, 0)
            d *= 2
        cnt = jnp.minimum(pos + 1.0, float(w))
        diff = cur[16:, :] / cnt - u_buf[16:16 + tl, cs]
        yb = _dot(diff.astype(BF16), w_pool_ref[gi]) * pscale_ref[:, cs]
        mix_buf[:, SSD_D_INNER + gi * POOL_GROUP_DIM:SSD_D_INNER + (gi + 1) * POOL_GROUP_DIM] = yb.astype(BF16)

    mix = _dot(mix_buf[...], w_out_ref[...])
    y_ref[0] = _layer_norm(ALPHA * x + mix, lng_ref[...], lnb_ref[...])

    lv = l_valid - (nt - 1) * tl

    @pl.when(j == nt - 1)
    def _():
        conv_o_ref[0] = xbc_buf[lv:lv + 8, :]
        pool_o_ref[0] = u_buf[lv:lv + 16, :]
        ssm_o_ref[0] = h_ref[...]

    @pl.when(j < nt - 1)
    def _():
        xbc_buf[0:8, :] = xbc_buf[tl:tl + 8, :]
        u_buf[0:16, :] = u_buf[tl:tl + 16, :]


def _const_spec(shape):
    nd = len(shape)
    return pl.BlockSpec(shape, lambda *_: (0,) * nd)


def _l0_mixer(x, conv0, ssm0_t, pool0, params, *, tl, l_valid, pos0):
    bsz, l_total, _ = x.shape
    nt = l_total // tl
    per_b3 = lambda b, j: (b, 0, 0)
    in_specs = [
        pl.BlockSpec((1, tl, D_MODEL), lambda b, j: (b, j, 0)),
        pl.BlockSpec((1, 8, SSD_CONV_DIM), per_b3),
        pl.BlockSpec((1, SSD_STATE, SSD_D_INNER), per_b3),
        pl.BlockSpec((1, 16, POOL_DIM), per_b3),
    ] + [_const_spec(p.shape) for p in params]
    out_shape = (
        jax.ShapeDtypeStruct((bsz, l_total, D_MODEL), F32),
        jax.ShapeDtypeStruct((bsz, 8, SSD_CONV_DIM), F32),
        jax.ShapeDtypeStruct((bsz, SSD_STATE, SSD_D_INNER), F32),
        jax.ShapeDtypeStruct((bsz, 16, POOL_DIM), F32),
    )
    out_specs = (
        pl.BlockSpec((1, tl, D_MODEL), lambda b, j: (b, j, 0)),
        pl.BlockSpec((1, 8, SSD_CONV_DIM), per_b3),
        pl.BlockSpec((1, SSD_STATE, SSD_D_INNER), per_b3),
        pl.BlockSpec((1, 16, POOL_DIM), per_b3),
    )
    scratch = [
        pltpu.VMEM((8 + tl, SSD_CONV_DIM), F32),
        pltpu.VMEM((16 + tl, POOL_DIM), F32),
        pltpu.VMEM((tl, SSD_CONV_DIM), F32),
        pltpu.VMEM((tl, SSD_D_INNER), F32),
        pltpu.VMEM((tl, SSD_D_INNER), F32),
        pltpu.VMEM((SSD_STATE, SSD_D_INNER), F32),
        pltpu.VMEM((tl, 2 * D_MODEL), BF16),
    ]
    return pl.pallas_call(
        functools.partial(_l0_kernel, tl=tl, l_total=l_total, l_valid=l_valid, pos0=pos0),
        grid=(bsz, nt),
        in_specs=in_specs,
        out_specs=out_specs,
        out_shape=out_shape,
        scratch_shapes=scratch,
        compiler_params=pltpu.CompilerParams(
            dimension_semantics=("arbitrary", "arbitrary"), vmem_limit_bytes=VMEM_LIMIT),
        name="l0_mixer",
    )(x, conv0, ssm0_t, pool0, *params)


def _l0_params(w_in, conv_w, conv_b, dt_bias, a_log, d_skip, ssd_norm, w_pool, pool_scale, w_out, ln_g, ln_b):
    rep = lambda v: jnp.repeat(v.astype(F32), SSD_HEAD_DIM)[None, :]
    z0, z1 = 0, SSD_D_INNER
    x1 = z1 + SSD_CONV_DIM
    d1 = x1 + SSD_HEADS
    w_z, w_xbc, w_dt, w_u = w_in[:, z0:z1], w_in[:, z1:x1], w_in[:, x1:d1], w_in[:, d1:]
    w_dtx = jnp.repeat(w_dt, SSD_HEAD_DIM, axis=1)
    w_all = jnp.concatenate([w_z, w_xbc, w_u, w_dtx], axis=1).astype(BF16)
    perm = jnp.concatenate([jnp.arange(0, SSD_HEADS, 2), jnp.arange(1, SSD_HEADS, 2)])
    a = -jnp.exp(a_log.astype(F32))
    return (
        w_all,
        w_dt.T[perm].astype(BF16),
        conv_w.astype(F32), conv_b.astype(F32)[None, :],
        rep(dt_bias), dt_bias.astype(F32)[perm][:, None],
        rep(a), a[perm][:, None],
        rep(d_skip), ssd_norm.astype(F32)[None, :],
        w_pool.astype(BF16), pool_scale.astype(F32)[None, :],
        w_out.astype(BF16), ln_g.astype(F32)[None, :], ln_b.astype(F32)[None, :],
    )


def _l1_kernel(x_ref, st0_ref, w_in_ref, lb_ref, normw_ref, w_out_ref, lng_ref, lnb_ref,
               y_ref, st_o_ref,
               q_buf, k_buf, lf_buf, v_buf, gt_buf, g_buf, o_buf, s_ref, mix_buf,
               *, tl, l_total, l_valid):
    j = pl.program_id(1)
    nt = l_total // tl
    c = HGRN_C
    nc = tl // c
    nsub = c // HGRN_SUB
    masked = l_valid < l_total

    @pl.when(j == 0)
    def _():
        s_ref[...] = st0_ref[0]

    x = x_ref[0]
    xb = x.astype(BF16)
    lb = lb_ref[...]
    q_raw = _dot(xb, w_in_ref[:, 0:1024])
    fr = _dot(xb, w_in_ref[:, 1024:2048])
    v_buf[...] = _dot(xb, w_in_ref[:, 2048:3072])
    gt_buf[...] = _silu(_dot(xb, w_in_ref[:, 3072:4096]))

    e = jnp.exp(-jnp.abs(fr))
    r = 1.0 / (1.0 + e)
    sig_pos = jnp.where(fr >= 0.0, r, e * r)
    sig_neg = jnp.where(fr >= 0.0, e * r, r)
    f = lb + (1.0 - lb) * sig_pos
    log_f = jnp.where(f > 0.0, jnp.log(f), jnp.log1p(-lb) + jnp.minimum(fr, 0.0))
    kk = (1.0 - lb) * sig_neg
    if masked:
        ok = (j * tl + _iota((tl, 1), 0)) < l_valid
        log_f = jnp.where(ok, log_f, 0.0)
        kk = jnp.where(ok, kk, 0.0)
    lf_buf[...] = log_f
    k_buf[...] = kk
    q_buf[...] = _silu(q_raw)

    ltri = (_iota((c, c), 1) <= _iota((c, c), 0)).astype(F32)
    s_col = _iota((HGRN_SUB, c), 1)
    l_row = _iota((HGRN_SUB, c), 0)
    normw = normw_ref[...]

    def chunk(ci, carry):
        r0 = pl.multiple_of(ci * c, c)
        rows = pl.ds(r0, c)
        g_all = _dot(ltri, lf_buf[rows, :], HIGHEST)
        g_buf[...] = g_all
        g_last = g_all[c - 1:c, :]
        q_c = q_buf[rows, :]
        k_c = k_buf[rows, :]
        qt = q_c * jnp.exp(g_all)
        kd = k_c * jnp.exp(g_last - g_all)
        spans = [(g_buf[lo - 1:lo, :] if lo else 0.0) - g_buf[lo + HGRN_SUB - 1:lo + HGRN_SUB, :]
                 for lo in range(0, c, HGRN_SUB)]
        worst_span = jnp.max(jnp.concatenate(spans, axis=0), keepdims=True)
        keep_diag = worst_span < HGRN_MAX_SPAN

        for h in range(HGRN_HEADS):
            hs = slice(h * HGRN_DK, (h + 1) * HGRN_DK)
            st = s_ref[h]
            v_h = v_buf[rows, hs].astype(BF16)
            o_h = _dot_nt(qt[:, hs].astype(BF16), st.astype(BF16))
            sc_rows = []
            for si in range(nsub):
                lo, hi = si * HGRN_SUB, (si + 1) * HGRN_SUB
                g_sub = g_buf[lo:hi, hs]
                if si == 0:
                    a_i = q_c[lo:hi, hs] * jnp.exp(g_sub)
                    b_i = k_c[0:hi, hs] * jnp.exp(-g_buf[0:hi, hs])
                else:
                    g_ref_row = g_buf[lo - 1:lo, hs]
                    a_i = q_c[lo:hi, hs] * jnp.exp(g_sub - g_ref_row)
                    b_i = k_c[0:hi, hs] * jnp.exp(g_ref_row - g_buf[0:hi, hs])
                b_i = b_i.astype(BF16)
                if hi < c:
                    b_i = jnp.concatenate([b_i, jnp.zeros((c - hi, HGRN_DK), BF16)], axis=0)
                sc = _dot_nt(a_i.astype(BF16), b_i)
                usable = (s_col <= l_row + lo) & ((s_col < lo) | keep_diag)
                sc_rows.append(jnp.where(usable, sc, 0.0))
            scores = jnp.concatenate(sc_rows, axis=0).astype(BF16)
            o_buf[:, hs] = o_h + _dot(scores, v_h)
            s_ref[h] = st * jnp.exp(g_last[:, hs]) + _dot_tn(v_h, kd[:, hs].astype(BF16))

        @pl.when(jnp.max(worst_span) >= HGRN_MAX_SPAN)
        def _():
            def exact_block(n, carry2):
                sb = pl.multiple_of((n % nsub) * HGRN_SUB, HGRN_SUB)
                col = pl.multiple_of((n // nsub) * HGRN_DK, HGRN_DK)
                blk, hcols = pl.ds(r0 + sb, HGRN_SUB), pl.ds(col, HGRN_DK)
                q_b, k_b, v_b = q_buf[blk, hcols], k_buf[blk, hcols], v_buf[blk, hcols]
                g_b = g_buf[pl.ds(sb, HGRN_SUB), hcols]
                srow = _iota((HGRN_SUB, 1), 0)
                acc = o_buf[pl.ds(sb, HGRN_SUB), hcols]
                for li in range(HGRN_SUB):
                    seg = g_b[li:li + 1, :] - g_b
                    w = k_b * q_b[li:li + 1, :] * jnp.exp(jnp.where(srow <= li, seg, -jnp.inf))
                    sc_l = jnp.sum(w, axis=1, keepdims=True)
                    acc = acc + jnp.where(srow == li, jnp.sum(sc_l * v_b, axis=0, keepdims=True), 0.0)
                o_buf[pl.ds(sb, HGRN_SUB), hcols] = acc
                return carry2

            lax.fori_loop(0, HGRN_HEADS * nsub, exact_block, 0)

        for h in range(HGRN_HEADS):
            hs = slice(h * HGRN_DK, (h + 1) * HGRN_DK)
            o_h = o_buf[:, hs]
            ms = jnp.mean(o_h * o_h, axis=-1, keepdims=True)
            mix_buf[rows, hs] = (o_h * lax.rsqrt(ms + EPS) * normw * gt_buf[rows, hs]).astype(BF16)
        return carry

    lax.fori_loop(0, nc, chunk, 0, unroll=True)

    mix = _dot(mix_buf[...], w_out_ref[...])
    y_ref[0] = _layer_norm(ALPHA * x + mix, lng_ref[...], lnb_ref[...])

    @pl.when(j == nt - 1)
    def _():
        st_o_ref[0] = s_ref[...]


def _l1_mixer(x, st0_t, params, *, tl, l_valid):
    bsz, l_total, _ = x.shape
    nt = l_total // tl
    st_spec = pl.BlockSpec((1, HGRN_HEADS, HGRN_DV, HGRN_DK), lambda b, j: (b, 0, 0, 0))
    in_specs = [pl.BlockSpec((1, tl, D_MODEL), lambda b, j: (b, j, 0)), st_spec]
    in_specs += [_const_spec(p.shape) for p in params]
    big = lambda: pltpu.VMEM((tl, D_MODEL), F32)
    return pl.pallas_call(
        functools.partial(_l1_kernel, tl=tl, l_total=l_total, l_valid=l_valid),
        grid=(bsz, nt),
        in_specs=in_specs,
        out_specs=(pl.BlockSpec((1, tl, D_MODEL), lambda b, j: (b, j, 0)), st_spec),
        out_shape=(jax.ShapeDtypeStruct((bsz, l_total, D_MODEL), F32),
                   jax.ShapeDtypeStruct((bsz, HGRN_HEADS, HGRN_DV, HGRN_DK), F32)),
        scratch_shapes=[big(), big(), big(), big(), big(),
                        pltpu.VMEM((HGRN_C, D_MODEL), F32),
                        pltpu.VMEM((HGRN_C, D_MODEL), F32),
                        pltpu.VMEM((HGRN_HEADS, HGRN_DV, HGRN_DK), F32),
                        pltpu.VMEM((tl, D_MODEL), BF16)],
        compiler_params=pltpu.CompilerParams(
            dimension_semantics=("arbitrary", "arbitrary"), vmem_limit_bytes=VMEM_LIMIT),
        name="l1_mixer",
    )(x, st0_t, *params)


def _l1_params(w_in, lb, hgrn_norm, w_out, ln_g, ln_b):
    return (w_in.astype(BF16), lb.astype(F32)[None, :], hgrn_norm.astype(F32)[None, :],
            w_out.astype(BF16), ln_g.astype(F32)[None, :], ln_b.astype(F32)[None, :])


def _route_kernel(x_ref, wrt_ref, bias_ref, idx_ref, gate_ref, pos_ref, cnt_ref, count_sc, *, tt):
    i = pl.program_id(0)

    @pl.when(i == 0)
    def _():
        count_sc[...] = jnp.zeros_like(count_sc)

    neg = -jnp.inf
    scores = jax.nn.sigmoid(_dot_nt(wrt_ref[...], x_ref[...], HIGHEST))
    sel3 = (scores + bias_ref[...]).reshape(N_EXPERT_GROUPS, GROUP_SIZE, tt)

    im = _iota(sel3.shape, 1)
    m1 = jnp.max(sel3, axis=1, keepdims=True)
    i1 = jnp.min(jnp.where(sel3 == m1, im, GROUP_SIZE), axis=1, keepdims=True)
    m2 = jnp.max(jnp.where(im == i1, neg, sel3), axis=1, keepdims=True)
    grp = (m1 + m2).reshape(N_EXPERT_GROUPS, tt)

    ig = _iota(grp.shape, 0)
    keep = jnp.zeros(grp.shape, F32)
    for _ in range(TOPK_GROUPS):
        gm = jnp.max(grp, axis=0, keepdims=True)
        hit = ig == jnp.min(jnp.where(grp == gm, ig, N_EXPERT_GROUPS), axis=0, keepdims=True)
        keep = jnp.where(hit, 1.0, keep)
        grp = jnp.where(hit, neg, grp)

    selm = jnp.where(keep.reshape(N_EXPERT_GROUPS, 1, tt) > 0.0, sel3, neg).reshape(N_EXPERTS, tt)
    ie = _iota(selm.shape, 0)
    membf = jnp.zeros(selm.shape, F32)
    idx_rows, w_rows = [], []
    for _ in range(TOP_K):
        m = jnp.max(selm, axis=0, keepdims=True)
        ei = jnp.min(jnp.where(selm == m, ie, N_EXPERTS), axis=0, keepdims=True)
        hit = ie == ei
        idx_rows.append(ei)
        w_rows.append(jnp.sum(jnp.where(hit, scores, 0.0), axis=0, keepdims=True))
        selm = jnp.where(hit, neg, selm)
        membf = jnp.where(hit, 1.0, membf)
    idx = jnp.concatenate(idx_rows, axis=0)
    w = jnp.concatenate(w_rows, axis=0)
    gate_ref[...] = w / jnp.sum(w, axis=0, keepdims=True) * ROUTED_SCALE
    idx_ref[...] = idx

    before =(_iota((tt, tt), 0) < _iota((tt, tt), 1)).astype(BF16)
    pos_all = _dot(membf.astype(BF16), before) + count_sc[:, 0:1]
    pos_rows = [jnp.sum(jnp.where(ie == idx_rows[k], pos_all, 0.0), axis=0, keepdims=True)
                for k in range(TOP_K)]
    pos_ref[...] = jnp.concatenate(pos_rows, axis=0).astype(I32)
    count_sc[...] = count_sc[...] + jnp.sum(membf, axis=1, keepdims=True)
    cnt_ref[...] = count_sc[...].astype(I32)


def _route(x2, w_r, b_r, *, tt):
    t_total = x2.shape[0]
    row_spec = pl.BlockSpec((TOP_K, tt), lambda i: (0, i))
    return pl.pallas_call(
        functools.partial(_route_kernel, tt=tt),
        grid=(t_total // tt,),
        in_specs=[pl.BlockSpec((tt, D_MODEL), lambda i: (i, 0)),
                  _const_spec((N_EXPERTS, D_MODEL)), _const_spec((N_EXPERTS, 1))],
        out_specs=(row_spec, row_spec, row_spec, _const_spec((N_EXPERTS, LANES))),
        out_shape=(jax.ShapeDtypeStruct((TOP_K, t_total), I32),
                   jax.ShapeDtypeStruct((TOP_K, t_total), F32),
                   jax.ShapeDtypeStruct((TOP_K, t_total), I32),
                   jax.ShapeDtypeStruct((N_EXPERTS, LANES), I32)),
        scratch_shapes=[pltpu.VMEM((N_EXPERTS, LANES), F32)],
        compiler_params=pltpu.CompilerParams(dimension_semantics=("arbitrary",), vmem_limit_bytes=VMEM_LIMIT),
        name="moe_route",
    )(x2, w_r.astype(F32).T, b_r.astype(F32)[:, None])


HIGH_HALF = 0xFFFF0000


def _pack_rows(dst_ref, val):
    n = val.shape[0]
    bits = lax.bitcast_convert_type(val.astype(BF16).astype(F32), U32)
    words = (bits[:, HALF_D:] & jnp.uint32(HIGH_HALF)) | (bits[:, :HALF_D] >> 16)
    for jj in range(PACK_ROWS):
        dst_ref[pl.ds(jj, n, stride=PACK_ROWS), :] = words[:, jj * LANES:(jj + 1) * LANES]


def _unpack_rows(src_ref):
    n = src_ref.shape[0] // PACK_ROWS
    words = [src_ref[pl.ds(jj, n, stride=PACK_ROWS), :] for jj in range(PACK_ROWS)]
    low = [lax.bitcast_convert_type(w << 16, F32) for w in words]
    high = [lax.bitcast_convert_type(w & jnp.uint32(HIGH_HALF), F32) for w in words]
    return jnp.concatenate(low + high, axis=1)


def _packed_row(r):
    return pl.ds(pl.multiple_of(r * PACK_ROWS, PACK_ROWS), PACK_ROWS)


def _dispatch_kernel(dest_ref, x_ref, xs_hbm, xp, sem, *, tt):
    _pack_rows(xp, x_ref[...])

    def issue(t, carry):
        for k in range(TOP_K):
            pltpu.make_async_copy(xp.at[_packed_row(t)], xs_hbm.at[_packed_row(dest_ref[t * TOP_K + k])],
                                  sem).start(priority=k % 2)
        return carry

    lax.fori_loop(0, tt, issue, 0)
    for _ in range(TOP_K):
        pltpu.make_async_copy(xp, xs_hbm.at[pl.ds(0, tt * PACK_ROWS)], sem).wait()


def _dispatch(x2, dest_flat, n_rows, *, tt):
    t_total = x2.shape[0]
    return pl.pallas_call(
        functools.partial(_dispatch_kernel, tt=tt),
        grid=(t_total // tt,),
        in_specs=[pl.BlockSpec((tt * TOP_K,), lambda i: (i,), memory_space=pltpu.SMEM),
                  pl.BlockSpec((tt, D_MODEL), lambda i: (i, 0))],
        out_specs=pl.BlockSpec(memory_space=pl.ANY),
        out_shape=jax.ShapeDtypeStruct((n_rows * PACK_ROWS, LANES), U32),
        scratch_shapes=[pltpu.VMEM((tt * PACK_ROWS, LANES), U32), pltpu.SemaphoreType.DMA(())],
        compiler_params=pltpu.CompilerParams(dimension_semantics=("arbitrary",), vmem_limit_bytes=VMEM_LIMIT),
        name="moe_dispatch",
    )(dest_flat, x2)


def _experts_kernel(blk_e_ref, blk_rows_ref, n_used_ref, xs_ref, w_in_ref, w_out_ref, ys_ref,
                    w_in_bf, w_out_bf, *, bm):
    i = pl.program_id(0)

    @pl.when(i < n_used_ref[0])
    def _():
        @pl.when((i == 0) | (blk_e_ref[i] != blk_e_ref[jnp.maximum(i - 1, 0)]))
        def _():
            w_in_bf[...] = w_in_ref[0, 0].astype(BF16)
            w_out_bf[...] = w_out_ref[0, 0].astype(BF16)

        live = _iota((bm, 1), 0) < blk_rows_ref[i]
        xb = jnp.where(live, _unpack_rows(xs_ref), 0.0).astype(BF16)
        h = _dot(xb, w_in_bf[...])
        act = (_silu(h[:, 0:EXPERT_FF]) * h[:, EXPERT_FF:2 * EXPERT_FF]).astype(BF16)
        _pack_rows(ys_ref, _dot(act, w_out_bf[...]))


def _experts(xs, blk_e, blk_rows, n_used, w_in, w_out, layer, *, bm):
    n_rows = xs.shape[0] // PACK_ROWS
    n_blocks = n_rows // bm
    last = lambda i, be, br, nu: jnp.minimum(i, nu[0] - 1)
    row_spec = pl.BlockSpec((bm * PACK_ROWS, LANES), lambda i, be, br, nu: (last(i, be, br, nu), 0))
    grid_spec = pltpu.PrefetchScalarGridSpec(
        num_scalar_prefetch=3,
        grid=(n_blocks,),
        in_specs=[row_spec,
                  pl.BlockSpec((1, 1, D_MODEL, 2 * EXPERT_FF),
                               lambda i, be, br, nu: (layer, be[last(i, be, br, nu)], 0, 0)),
                  pl.BlockSpec((1, 1, EXPERT_FF, D_MODEL),
                               lambda i, be, br, nu: (layer, be[last(i, be, br, nu)], 0, 0))],
        out_specs=row_spec,
        scratch_shapes=[pltpu.VMEM((D_MODEL, 2 * EXPERT_FF), BF16), pltpu.VMEM((EXPERT_FF, D_MODEL), BF16)],
    )
    return pl.pallas_call(
        functools.partial(_experts_kernel, bm=bm),
        grid_spec=grid_spec,
        out_shape=jax.ShapeDtypeStruct((n_rows * PACK_ROWS, LANES), U32),
        compiler_params=pltpu.CompilerParams(dimension_semantics=("arbitrary",), vmem_limit_bytes=VMEM_LIMIT),
        name="moe_experts",
    )(blk_e, blk_rows, n_used, xs, w_in, w_out)


def _combine_kernel(dest_ref, dest_next_ref, x_ref, gate_ref, ys_hbm, ws_in_ref, ws_out_ref, lng_ref, lnb_ref,
                    y_ref, buf_even, buf_odd, shared_buf, sems, *, tt, n_tiles):
    i = pl.program_id(0)
    grp = min(COMBINE_GROUP, tt)
    n_groups = tt // grp

    def request_rows(d_ref, buf, sem, t0):
        for tj in range(grp):
            for k in range(TOP_K):
                pltpu.make_async_copy(ys_hbm.at[_packed_row(d_ref[(t0 + tj) * TOP_K + k])],
                                      buf.at[k, _packed_row(t0 + tj)],
                                      sem).start(priority=k % 2)

    def combine_rows(buf, t0):
        rows = pl.ds(t0, grp)
        gates = gate_ref[rows, :]
        packed = pl.ds(pl.multiple_of(t0 * PACK_ROWS, grp * PACK_ROWS), grp * PACK_ROWS)
        routed = _unpack_rows(buf.at[0, packed]) * gates[:, 0:1]
        for k in range(1, TOP_K):
            routed = routed + _unpack_rows(buf.at[k, packed]) * gates[:, k:k + 1]
        y_ref[rows, :] = _layer_norm(ALPHA * x_ref[rows, :] + (routed + shared_buf[rows, :]),
                                     lng_ref[...], lnb_ref[...])

    def first_requests(g, carry):
        request_rows(dest_ref, buf_even, sems.at[0], pl.multiple_of(g * grp, grp))
        return carry

    @pl.when(i == 0)
    def _():
        lax.fori_loop(0, n_groups, first_requests, 0)

    h = _dot(x_ref[...].astype(BF16), ws_in_ref[...])
    act = (_silu(h[:, 0:EXPERT_FF]) * h[:, EXPERT_FF:2 * EXPERT_FF]).astype(BF16)
    shared_buf[...] = _dot(act, ws_out_ref[...])

    def step(cur, cur_sem, nxt, nxt_sem):
        for k in range(TOP_K):
            pltpu.make_async_copy(ys_hbm.at[pl.ds(0, tt * PACK_ROWS)], cur.at[k], cur_sem).wait()

        def requests_and_combine(g, carry):
            t0 = pl.multiple_of(g * grp, grp)
            request_rows(dest_next_ref, nxt, nxt_sem, t0)
            combine_rows(cur, t0)
            return carry

        def combine_only(g, carry):
            combine_rows(cur, pl.multiple_of(g * grp, grp))
            return carry

        @pl.when(i + 1 < n_tiles)
        def _():
            lax.fori_loop(0, n_groups, requests_and_combine, 0)

        @pl.when(i + 1 >= n_tiles)
        def _():
            lax.fori_loop(0, n_groups, combine_only, 0)

    @pl.when(i % 2 == 0)
    def _():
        step(buf_even, sems.at[0], buf_odd, sems.at[1])

    @pl.when(i % 2 == 1)
    def _():
        step(buf_odd, sems.at[1], buf_even, sems.at[0])


def _combine(x2, dest_flat, gates, ys, ws_in, ws_out, ln_g, ln_b, *, tt):
    t_total = x2.shape[0]
    n_tiles = t_total // tt
    return pl.pallas_call(
        functools.partial(_combine_kernel, tt=tt, n_tiles=n_tiles),
        grid=(n_tiles,),
        in_specs=[pl.BlockSpec((tt * TOP_K,), lambda i: (i,), memory_space=pltpu.SMEM),
                  pl.BlockSpec((tt * TOP_K,), lambda i: (jnp.minimum(i + 1, n_tiles - 1),), memory_space=pltpu.SMEM),
                  pl.BlockSpec((tt, D_MODEL), lambda i: (i, 0)),
                  pl.BlockSpec((tt, TOP_K), lambda i: (i, 0)),
                  pl.BlockSpec(memory_space=pl.ANY),
                  _const_spec((D_MODEL, 2 * EXPERT_FF)), _const_spec((EXPERT_FF, D_MODEL)),
                  _const_spec((1, D_MODEL)), _const_spec((1, D_MODEL))],
        out_specs=pl.BlockSpec((tt, D_MODEL), lambda i: (i, 0)),
        out_shape=jax.ShapeDtypeStruct((t_total, D_MODEL), F32),
        scratch_shapes=[pltpu.VMEM((TOP_K, tt * PACK_ROWS, LANES), U32), pltpu.VMEM((TOP_K, tt * PACK_ROWS, LANES), U32),
                        pltpu.VMEM((tt, D_MODEL), F32), pltpu.SemaphoreType.DMA((2,))],
        compiler_params=pltpu.CompilerParams(dimension_semantics=("arbitrary",), vmem_limit_bytes=VMEM_LIMIT),
        name="moe_combine",
    )(dest_flat, dest_flat, x2, gates, ys, ws_in.astype(BF16), ws_out.astype(BF16),
      ln_g.astype(F32)[None, :], ln_b.astype(F32)[None, :])


def _moe_ffn(x, w_r, b_r, w_in, w_out, layer, ws_in, ws_out, ln_g, ln_b, *, tt, bm):
    bsz, l_total, _ = x.shape
    t_total = bsz * l_total
    x2 = x.reshape(t_total, D_MODEL)
    idx_t, gate_t, pos_t, cnt = _route(x2, w_r, b_r, tt=min(ROUTE_TILE, t_total))

    counts = cnt[:, 0]
    padded = (counts + bm - 1) // bm * bm
    pad_end = jnp.cumsum(padded)
    pad_start = pad_end - padded
    onehot = idx_t[:, :, None] == jnp.arange(N_EXPERTS, dtype=I32)
    dest_t = pos_t + jnp.sum(jnp.where(onehot, pad_start, 0), axis=-1)
    dest_flat = dest_t.T.reshape(-1)
    n_blocks = -(-(t_total * TOP_K + N_EXPERTS * (bm - 1)) // bm)
    blk_start = jnp.arange(n_blocks, dtype=I32) * bm
    blk_e = jnp.sum((pad_end[None, :] <= blk_start[:, None]).astype(I32), axis=1)
    blk_e = jnp.minimum(blk_e, N_EXPERTS - 1)
    blk_rows = jnp.clip(pad_start[blk_e] + counts[blk_e] - blk_start, 0, bm).astype(I32)
    n_used = (pad_end[-1:] // bm).astype(I32)

    tt_d = min(tt, t_total)
    xs = _dispatch(x2, dest_flat, n_blocks * bm, tt=tt_d)
    ys = _experts(xs, blk_e, blk_rows, n_used, w_in.astype(F32), w_out.astype(F32), layer, bm=bm)
    y = _combine(x2, dest_flat, gate_t.T, ys, ws_in, ws_out, ln_g, ln_b, tt=tt_d)
    return y.reshape(bsz, l_total, D_MODEL)


def _pad_time(t, total):
    return jnp.pad(t, [(0, 0), (0, total - t.shape[1]), (0, 0)])


def _trunk(x, pos0, conv_st, ssm_st, pool_st, hgrn_st, weights, *, tl, tt, bm):
    (w_in_l0, conv_w_l0, conv_b_l0, dt_bias_l0, a_log_l0, d_skip_l0, ssd_norm_l0, w_pool_l0,
     pool_scale_l0, w_out_l0, w_in_l1, lb_raw, hgrn_norm_l1, w_out_l1, ln_mix_g, ln_mix_b,
     ln_ffn_g, ln_ffn_b, router_w, router_bias, moe_w_in, moe_w_out, shared_w_in, shared_w_out) = weights
    bsz, l_valid, _ = x.shape
    l_total = -(-l_valid // tl) * tl
    xp = _pad_time(x, l_total)

    lb_sm = jax.nn.softmax(lb_raw.astype(F32), axis=0)
    lb_all = jnp.cumsum(lb_sm, axis=0) - lb_sm[0]

    conv0 = jnp.pad(conv_st.astype(F32), [(0, 0), (8 - (SSD_CONV_W - 1), 0), (0, 0)])
    pool0 = jnp.pad(pool_st.astype(F32), [(0, 0), (1, 0), (0, 0)])
    ssm0_t = ssm_st.astype(F32).reshape(bsz, SSD_D_INNER, SSD_STATE).transpose(0, 2, 1)
    p0 = _l0_params(w_in_l0, conv_w_l0, conv_b_l0, dt_bias_l0, a_log_l0, d_skip_l0, ssd_norm_l0,
                    w_pool_l0, pool_scale_l0, w_out_l0, ln_mix_g[0], ln_mix_b[0])
    h, conv_o, ssm_o, pool_o = _l0_mixer(xp, conv0, ssm0_t, pool0, p0, tl=tl, l_valid=l_valid, pos0=pos0)
    new_conv = conv_o[:, 8 - (SSD_CONV_W - 1):]
    new_pool = pool_o[:, 1:]
    new_ssm = ssm_o.transpose(0, 2, 1).reshape(bsz, SSD_HEADS, SSD_HEAD_DIM, SSD_STATE)
    h = _moe_ffn(h[:, :l_valid], router_w[0], router_bias[0], moe_w_in, moe_w_out, 0, shared_w_in[0],
                 shared_w_out[0], ln_ffn_g[0], ln_ffn_b[0], tt=tt, bm=bm)

    p1 = _l1_params(w_in_l1, lb_all[1], hgrn_norm_l1, w_out_l1, ln_mix_g[1], ln_mix_b[1])
    h, hgrn_o = _l1_mixer(_pad_time(h, l_total), hgrn_st.astype(F32).transpose(0, 1, 3, 2), p1,
                          tl=tl, l_valid=l_valid)
    new_hgrn = hgrn_o.transpose(0, 1, 3, 2)
    h = _moe_ffn(h[:, :l_valid], router_w[1], router_bias[1], moe_w_in, moe_w_out, 1, shared_w_in[1],
                 shared_w_out[1], ln_ffn_g[1], ln_ffn_b[1], tt=tt, bm=bm)
    return h, new_conv, new_ssm, new_pool, new_hgrn


PROMPT_TL = 256
SAMPLE_TL = 128
TOKEN_TILE = 256
EXPERT_BLOCK = 1024
SAMPLE_EXPERT_BLOCK = 128


def kernel(x_prompt, x_sample, state_conv_l0, state_ssm_l0, state_pool_l0, state_hgrn_l1, w_in_l0, conv_w_l0, conv_b_l0, dt_bias_l0, a_log_l0, d_skip_l0, ssd_norm_l0, w_pool_l0, pool_scale_l0, w_out_l0, w_in_l1, lb_raw, hgrn_norm_l1, w_out_l1, ln_mix_g, ln_mix_b, ln_ffn_g, ln_ffn_b, router_w, router_bias, moe_w_in, moe_w_out, shared_w_in, shared_w_out):
    weights = (w_in_l0, conv_w_l0, conv_b_l0, dt_bias_l0, a_log_l0, d_skip_l0, ssd_norm_l0, w_pool_l0,
               pool_scale_l0, w_out_l0, w_in_l1, lb_raw, hgrn_norm_l1, w_out_l1, ln_mix_g, ln_mix_b,
               ln_ffn_g, ln_ffn_b, router_w, router_bias, moe_w_in, moe_w_out, shared_w_in, shared_w_out)
    bp = x_prompt.shape[0]
    past_len = 1024
    zeros = lambda *s: jnp.zeros(s, F32)
    y_p, p_conv, p_ssm, p_pool, p_hgrn = _trunk(
        x_prompt, 0, zeros(bp, SSD_CONV_W - 1, SSD_CONV_DIM), zeros(bp, SSD_HEADS, SSD_HEAD_DIM, SSD_STATE),
        zeros(bp, POOL_MAXW - 1, POOL_DIM), zeros(bp, HGRN_HEADS, HGRN_DK, HGRN_DV), weights,
        tl=PROMPT_TL, tt=TOKEN_TILE, bm=EXPERT_BLOCK)
    y_s, s_conv, s_ssm, s_pool, s_hgrn = _trunk(
        x_sample, past_len, state_conv_l0, state_ssm_l0, state_pool_l0, state_hgrn_l1, weights,
        tl=SAMPLE_TL, tt=TOKEN_TILE, bm=SAMPLE_EXPERT_BLOCK)
    return (y_p, y_s, p_conv, p_ssm, p_pool, p_hgrn, s_conv, s_ssm, s_pool, s_hgrn)
```
